```python
import jax
import jax.numpy as jnp
from jax import lax
import numpy as np

D_MODEL = 2048
BATCH = 4
SEQ = 2048
DEPTH = 4
DEC_BATCH = 8
DEC_SEQ = 8
PAST_LEN = 16384
PAGE_SIZE = 128

NORM_EPS = 1e-6
N_AB = (DEPTH + 1) // 2
N_C = DEPTH // 2
A_GROUPS = ((128, 1), (512, 4), (2048, 16))
A_HEADS = 8
A_HEAD_DIM = 128
A_WIDTH = A_HEADS * A_HEAD_DIM
B_HEADS = 8
B_KEY_DIM = 128
B_VAL_DIM = 128
B_WIDTH = B_HEADS * B_VAL_DIM
B_CHUNK = 64
AB_SIZES = (A_WIDTH,) * (3 * len(A_GROUPS)) + (B_HEADS * B_KEY_DIM,) * 2 + (B_WIDTH,) * 2
AB_IN = sum(AB_SIZES)
C_HEAD_DIM = 64
C_HEADS = D_MODEL // C_HEAD_DIM
C_DECAY_LORA = max(32, int(round(1.8 * D_MODEL ** 0.5 / 32)) * 32)
C_ICLR_LORA = max(32, int(round(1.8 * D_MODEL ** 0.5 / 32)) * 32)
C_GATE_LORA = max(32, int(round(0.6 * D_MODEL ** 0.8 / 32)) * 32)
C_GN_EPS = 64e-5
FFN_HIDDEN = -((-8 * D_MODEL) // (3 * 256)) * 256

kernel_name = 'dilated_hgrn2_rwkv7_hybrid_step'


def _rmsnorm(x, gain):
    xf = x.astype(jnp.float32)
    y = xf * lax.rsqrt(jnp.mean(xf * xf, axis=-1, keepdims=True) + NORM_EPS)
    return (y * gain.astype(jnp.float32)).astype(x.dtype)


def _alibi_slopes(n):
    return jnp.exp2(-8.0 * jnp.arange(1, n + 1, dtype=jnp.float32) / n)


def _dilated_window_prompt(q, k, v, window, dilation, slopes):
    bn, t_len, nh, hd = q.shape
    nk = window // dilation
    m_len = t_len // dilation
    blk = nk
    nb = -(-m_len // blk)
    m_pad = nb * blk

    def to_residue_blocks(t):
        t = t.reshape(bn, m_len, dilation, nh, hd).transpose(0, 2, 1, 3, 4)
        t = jnp.pad(t, ((0, 0), (0, 0), (0, m_pad - m_len), (0, 0), (0, 0)))
        return t.reshape(bn, dilation, nb, blk, nh, hd)

    def with_previous(t):
        prev = jnp.pad(t, ((0, 0), (0, 0), (1, 0), (0, 0), (0, 0), (0, 0)))[:, :, :-1]
        return jnp.concatenate([prev, t], axis=3)

    qb = to_residue_blocks(q)
    kb = with_previous(to_residue_blocks(k))
    vb = with_previous(to_residue_blocks(v))
    qi = jnp.arange(blk)[:, None] + blk
    ki = jnp.arange(2 * blk)[None, :]
    dist = qi - ki
    key_m = (jnp.arange(nb) * blk)[:, None, None] + ki[None] - blk
    valid = (dist >= 0)[None] & (dist <= nk)[None] & (key_m >= 0)
    s = jnp.einsum('brnqhd,brnkhd->brnhqk', qb, kb).astype(jnp.float32) * (hd ** -0.5)
    s = s - slopes[:, None, None] * (dist * dilation).astype(jnp.float32)
    s = jnp.where(valid[:, None], s, -jnp.inf)
    mx = jnp.max(s, axis=-1, keepdims=True)
    p = jnp.exp(s - mx)
    den = jnp.sum(p, axis=-1, keepdims=True)
    o = jnp.einsum('brnhqk,brnkhd->brnqhd', p / den, vb.astype(jnp.float32))
    lse = (mx + jnp.log(den))[..., 0]
    o = o.reshape(bn, dilation, m_pad, nh, hd)[:, :, :m_len].transpose(0, 2, 1, 3, 4).reshape(bn, t_len, nh, hd)
    lse = lse.transpose(0, 1, 2, 4, 3).reshape(bn, dilation, m_pad, nh)[:, :, :m_len]
    lse = lse.transpose(0, 2, 1, 3).reshape(bn, t_len, nh)
    return o, lse


def _dilated_window_sample(q, k_all, v_all, window, dilation, slopes, n_buf):
    l_new, hd = q.shape[1], q.shape[-1]
    nk = window // dilation
    steps = jnp.arange(nk + 1)
    idx = n_buf + jnp.arange(l_new)[:, None] - steps[None, :] * dilation
    valid = idx >= 0
    idx = jnp.maximum(idx, 0)
    kg = k_all[:, idx]
    vg = v_all[:, idx]
    s = jnp.einsum('blhd,bljhd->blhj', q, kg).astype(jnp.float32) * (hd ** -0.5)
    s = s - slopes[:, None] * (steps * dilation).astype(jnp.float32)[None]
    s = jnp.where(valid[:, None, :], s, -jnp.inf)
    mx = jnp.max(s, axis=-1, keepdims=True)
    p = jnp.exp(s - mx)
    den = jnp.sum(p, axis=-1, keepdims=True)
    o = jnp.einsum('blhj,bljhd->blhd', p / den, vg.astype(jnp.float32))
    lse = (mx + jnp.log(den))[..., 0]
    return o, lse


def _merge_by_denominator(outs, lses):
    w = jax.nn.softmax(jnp.stack(lses, axis=0), axis=0)
    return jnp.einsum('gbth,gbthd->bthd', w, jnp.stack(outs, axis=0))


def _gla_chunked(q, k, v, g, s0, chunk):
    bn, t_len, nh, _ = q.shape
    c = min(chunk, t_len)
    n = -(-t_len // c)
    t_pad = n * c

    def to_chunks(t):
        t = jnp.pad(t.astype(jnp.float32), ((0, 0), (0, t_pad - t_len), (0, 0), (0, 0)))
        return t.reshape(bn, n, c, nh, -1).transpose(1, 0, 3, 2, 4)

    qc, kc, vc, gc = (to_chunks(t) for t in (q, k, v, g))
    causal = jnp.tril(jnp.ones((c, c), dtype=bool))[:, :, None]

    def step(s, inp):
        qi, ki, vi, gi = inp
        b = jnp.cumsum(gi, axis=2)
        o_inter = jnp.einsum('bhtk,bhkv->bhtv', qi * jnp.exp(b), s)
        diff = b[:, :, :, None, :] - b[:, :, None, :, :]
        decay = jnp.where(causal, jnp.exp(jnp.where(causal, diff, 0.0)), 0.0)
        att = jnp.einsum('bhtk,bhtsk,bhsk->bhts', qi, decay, ki)
        o = o_inter + jnp.einsum('bhts,bhsv->bhtv', att, vi)
        b_last = b[:, :, -1:, :]
        s_new = jnp.exp(b_last[:, :, 0, :, None]) * s + jnp.einsum('bhsk,bhsv->bhkv', ki * jnp.exp(b_last - b), vi)
        return s_new, o

    s_fin, o = lax.scan(step, s0.astype(jnp.float32), (qc, kc, vc, gc))
    o = o.transpose(1, 0, 3, 2, 4).reshape(bn, t_pad, nh, -1)[:, :t_len]
    return o, s_fin


def _ab_mixer(h, w_in, w_out, lower_bound, b_gain, a_bufs, b_state):
    bn, t_len, _ = h.shape
    proj = h @ w_in
    parts = jnp.split(proj, [int(c) for c in np.cumsum(AB_SIZES)[:-1]], axis=-1)
    slopes = _alibi_slopes(A_HEADS)
    outs, lses, new_bufs = [], [], []
    for gi, (window, dil) in enumerate(A_GROUPS):
        q, k, v = (t.reshape(bn, t_len, A_HEADS, A_HEAD_DIM) for t in parts[3 * gi:3 * gi + 3])
        kv_new = jnp.stack([k, v], axis=2)
        if a_bufs is None:
            o, lse = _dilated_window_prompt(q, k, v, window, dil, slopes)
            new_bufs.append(kv_new[:, -min(window, t_len):])
        else:
            n_buf = a_bufs[gi].shape[1]
            kv_all = jnp.concatenate([a_bufs[gi].astype(kv_new.dtype), kv_new], axis=1)
            o, lse = _dilated_window_sample(q, kv_all[:, :, 0], kv_all[:, :, 1], window, dil, slopes, n_buf)
            new_bufs.append(kv_all[:, t_len:])
        outs.append(o)
        lses.append(lse)
    a_out = _merge_by_denominator(outs, lses).reshape(bn, t_len, A_WIDTH).astype(h.dtype)

    bq, bf, bi, bg = parts[-4:]
    q = jax.nn.silu(bq.astype(jnp.float32)).reshape(bn, t_len, B_HEADS, B_KEY_DIM)
    lb = lower_bound.reshape(B_HEADS, B_KEY_DIM)
    fgate = lb + (1.0 - lb) * jax.nn.sigmoid(bf.astype(jnp.float32).reshape(bn, t_len, B_HEADS, B_KEY_DIM))
    o, s_b = _gla_chunked(q, 1.0 - fgate, bi.reshape(bn, t_len, B_HEADS, B_VAL_DIM), jnp.log(fgate), b_state, B_CHUNK)
    o = _rmsnorm(o, b_gain.reshape(B_HEADS, B_VAL_DIM)) * jax.nn.silu(bg.astype(jnp.float32).reshape(bn, t_len, B_HEADS, B_VAL_DIM))
    b_out = o.reshape(bn, t_len, B_WIDTH).astype(h.dtype)
    y = jnp.concatenate([a_out, b_out], axis=-1) @ w_out
    return y, tuple(new_bufs), s_b


def _rwkv7_scan(r, decay, k, v, kk, a, s0):
    seq = tuple(t.transpose(1, 0, 2, 3) for t in (r, decay, k, v, kk, a))

    def step(s, inp):
        r_t, w_t, k_t, v_t, kk_t, a_t = inp
        sa = jnp.einsum('bhvk,bhk->bhv', s, -kk_t)
        s = s * w_t[:, :, None, :] + sa[..., None] * (kk_t * a_t)[:, :, None, :] + v_t[..., None] * k_t[:, :, None, :]
        return s, jnp.einsum('bhvk,bhk->bhv', s, r_t)

    s_fin, o = lax.scan(step, s0.astype(jnp.float32), seq)
    return o.transpose(1, 0, 2, 3), s_fin


def _rwkv7_mixer(h, shift, wkv, c_mu, c_w_rkv, c_w0, c_w1, c_w2, c_a0, c_a1, c_a2, c_g1, c_g2,
                 c_k_k, c_k_a, c_r_k, c_ln_w, c_ln_b, c_w_out):
    bn, t_len, d = h.shape
    prev = jnp.concatenate([shift[:, None].astype(h.dtype), h[:, :-1]], axis=1)
    xx = prev - h
    xr, xw, xk, xv, xa, xg = (h + xx * c_mu[j] for j in range(6))
    r, k, v = jnp.einsum('nbtd,nde->nbte', jnp.stack([xr, xk, xv]), c_w_rkv)
    log_decay = -jnp.exp(-jax.nn.softplus(-(c_w0 + jnp.tanh(xw @ c_w1) @ c_w2).astype(jnp.float32)) - 0.5)
    a = jax.nn.sigmoid((c_a0 + (xa @ c_a1) @ c_a2).astype(jnp.float32))
    gate = (jax.nn.sigmoid(xg @ c_g1) @ c_g2).astype(jnp.float32)

    def heads(t):
        return t.astype(jnp.float32).reshape(bn, t_len, C_HEADS, C_HEAD_DIM)

    kk = heads(k * c_k_k)
    kk = kk / jnp.maximum(jnp.sqrt(jnp.sum(kk * kk, axis=-1, keepdims=True)), 1e-12)
    k2 = k.astype(jnp.float32) * (1.0 + (a - 1.0) * c_k_a.astype(jnp.float32))
    r, k2, v, a, decay = (heads(t) for t in (r, k2, v, a, jnp.exp(log_decay)))
    o, s_fin = _rwkv7_scan(r, decay, k2, v, kk, a, wkv)
    mean = jnp.mean(o, axis=-1, keepdims=True)
    var = jnp.mean(jnp.square(o - mean), axis=-1, keepdims=True)
    o = (o - mean) * lax.rsqrt(var + C_GN_EPS) * heads(c_ln_w) [0, 0] + heads(c_ln_b)[0, 0] if False else \
        (o - mean) * lax.rsqrt(var + C_GN_EPS) * c_ln_w.astype(jnp.float32).reshape(C_HEADS, C_HEAD_DIM) + c_ln_b.astype(jnp.float32).reshape(C_HEADS, C_HEAD_DIM)
    o = o + jnp.sum(r * k2 * c_r_k.astype(jnp.float32), axis=-1, keepdims=True) * v
    y = (o.reshape(bn, t_len, d) * gate).astype(h.dtype) @ c_w_out
    return y, s_fin, h[:, -1]


def _swiglu(h, w_up, w_down):
    gate, up = jnp.split(h @ w_up, 2, axis=-1)
    return (jax.nn.silu(gate) * up) @ w_down


C_PARAM_NAMES = ('c_mu', 'c_w_rkv', 'c_w0', 'c_w1', 'c_w2', 'c_a0', 'c_a1', 'c_a2', 'c_g1', 'c_g2',
                 'c_k_k', 'c_k_a', 'c_r_k', 'c_ln_w', 'c_ln_b', 'c_w_out')


def _trunk(x, a_caches, b_states, c_wkv, c_shift, p):
    bn = x.shape[0]
    lb_soft = jax.nn.softmax(p['b_lower_bounds'].astype(jnp.float32), axis=0)
    lower_bounds = jnp.cumsum(lb_soft, axis=0) - lb_soft[0]
    new_a = tuple([] for _ in A_GROUPS)
    new_b, new_wkv, new_shift = [], [], []
    for layer in range(DEPTH):
        gains = p['norm_gains'][layer]
        li = layer // 2
        h = _rmsnorm(x, gains[0])
        if layer % 2 == 0:
            bufs = None if a_caches is None else tuple(c[li] for c in a_caches)
            s0 = jnp.zeros((bn, B_HEADS, B_KEY_DIM, B_VAL_DIM), jnp.float32) if b_states is None else b_states[li]
            mix, kv_bufs, s_b = _ab_mixer(h, p['w_in_ab'][li], p['w_out_ab'][li], lower_bounds[li],
                                          p['b_norm_gain'][li], bufs, s0)
            for lst, kv in zip(new_a, kv_bufs):
                lst.append(kv)
            new_b.append(s_b)
        else:
            wkv0 = jnp.zeros((bn, C_HEADS, C_HEAD_DIM, C_HEAD_DIM), jnp.float32) if c_wkv is None else c_wkv[li]
            sh0 = jnp.zeros((bn, D_MODEL), x.dtype) if c_shift is None else c_shift[li]
            cp = {name: p[name][li] for name in C_PARAM_NAMES}
            mix, s_wkv, s_shift = _rwkv7_mixer(h, sh0, wkv0, **cp)
            new_wkv.append(s_wkv)
            new_shift.append(s_shift)
        x = x + _rmsnorm(mix, gains[1])
        h = _rmsnorm(x, gains[2])
        x = x + _rmsnorm(_swiglu(h, p['w_ffn_up'][layer], p['w_ffn_down'][layer]), gains[3])
    a_states = tuple(jnp.stack(lst) for lst in new_a)
    return x, a_states, jnp.stack(new_b), jnp.stack(new_wkv), jnp.stack(new_shift)


def setup_inputs(seed: int = 0) -> dict:
    key = jax.random.key(seed)
    keys = iter(jax.random.split(key, 48))

    def nrm(shape, scale):
        return jax.random.normal(next(keys), shape, jnp.float32) * scale

    def unif(shape, lo, hi):
        return jax.random.uniform(next(keys), shape, jnp.float32, lo, hi)

    d = D_MODEL
    inputs = {
        'x_prompt': nrm((BATCH, SEQ, d), 1.0),
        'x_sample': nrm((DEC_BATCH, DEC_SEQ, d), 1.0),
    }
    for gi, (window, _) in enumerate(A_GROUPS):
        inputs['cache_a%d_kv' % (gi + 1)] = nrm((N_AB, DEC_BATCH, min(window, PAST_LEN), 2, A_HEADS, A_HEAD_DIM), 1.0)
    inputs['state_b'] = nrm((N_AB, DEC_BATCH, B_HEADS, B_KEY_DIM, B_VAL_DIM), 0.5)
    inputs['state_c_wkv'] = nrm((N_C, DEC_BATCH, C_HEADS, C_HEAD_DIM, C_HEAD_DIM), 0.5)
    inputs['state_c_shift'] = nrm((N_C, DEC_BATCH, d), 1.0)
    inputs['norm_gains'] = 1.0 + nrm((DEPTH, 4, d), 0.05)
    inputs['w_in_ab'] = nrm((N_AB, d, AB_IN), d ** -0.5)
    inputs['w_out_ab'] = nrm((N_AB, A_WIDTH + B_WIDTH, d), (A_WIDTH + B_WIDTH) ** -0.5)
    inputs['b_lower_bounds'] = 1.0 + nrm((N_AB, B_HEADS * B_KEY_DIM), 0.3)
    inputs['b_norm_gain'] = 1.0 + nrm((N_AB, B_WIDTH), 0.05)
    inputs['c_mu'] = unif((N_C, 6, d), 0.0, 1.0)
    inputs['c_w_rkv'] = nrm((N_C, 3, d, d), d ** -0.5)
    inputs['c_w0'] = unif((N_C, d), -3.0, 1.0)
    inputs['c_w1'] = nrm((N_C, d, C_DECAY_LORA), d ** -0.5)
    inputs['c_w2'] = nrm((N_C, C_DECAY_LORA, d), 0.5 * C_DECAY_LORA ** -0.5)
    inputs['c_a0'] = nrm((N_C, d), 0.3)
    inputs['c_a1'] = nrm((N_C, d, C_ICLR_LORA), d ** -0.5)
    inputs['c_a2'] = nrm((N_C, C_ICLR_LORA, d), 0.5 * C_ICLR_LORA ** -0.5)
    inputs['c_g1'] = nrm((N_C, d, C_GATE_LORA), d ** -0.5)
    inputs['c_g2'] = nrm((N_C, C_GATE_LORA, d), C_GATE_LORA ** -0.5)
    inputs['c_k_k'] = 0.85 + nrm((N_C, d), 0.05)
    inputs['c_k_a'] = 1.0 + nrm((N_C, d), 0.05)
    inputs['c_r_k'] = nrm((N_C, C_HEADS, C_HEAD_DIM), 0.1)
    inputs['c_ln_w'] = 1.0 + nrm((N_C, d), 0.05)
    inputs['c_ln_b'] = nrm((N_C, d), 0.01)
    inputs['c_w_out'] = nrm((N_C, d, d), d ** -0.5)
    inputs['w_ffn_up'] = nrm((DEPTH, d, 2 * FFN_HIDDEN), d ** -0.5)
    inputs['w_ffn_down'] = nrm((DEPTH, FFN_HIDDEN, d), FFN_HIDDEN ** -0.5)
    return inputs


def reference(x_prompt, x_sample, cache_a1_kv, cache_a2_kv, cache_a3_kv, state_b, state_c_wkv, state_c_shift,
              norm_gains, w_in_ab, w_out_ab, b_lower_bounds, b_norm_gain, c_mu, c_w_rkv, c_w0, c_w1, c_w2,
              c_a0, c_a1, c_a2, c_g1, c_g2, c_k_k, c_k_a, c_r_k, c_ln_w, c_ln_b, c_w_out, w_ffn_up, w_ffn_down):
    p = dict(norm_gains=norm_gains, w_in_ab=w_in_ab, w_out_ab=w_out_ab, b_lower_bounds=b_lower_bounds,
             b_norm_gain=b_norm_gain, c_mu=c_mu, c_w_rkv=c_w_rkv, c_w0=c_w0, c_w1=c_w1, c_w2=c_w2,
             c_a0=c_a0, c_a1=c_a1, c_a2=c_a2, c_g1=c_g1, c_g2=c_g2, c_k_k=c_k_k, c_k_a=c_k_a, c_r_k=c_r_k,
             c_ln_w=c_ln_w, c_ln_b=c_ln_b, c_w_out=c_w_out, w_ffn_up=w_ffn_up, w_ffn_down=w_ffn_down)
    y_prompt, (pa1, pa2, pa3), pb, pwkv, pshift = _trunk(x_prompt, None, None, None, None, p)
    y_sample, (sa1, sa2, sa3), sb, swkv, sshift = _trunk(
        x_sample, (cache_a1_kv, cache_a2_kv, cache_a3_kv), state_b, state_c_wkv, state_c_shift, p)
    return (y_prompt, y_sample, pa1, pa2, pa3, pb, pwkv, pshift, sa1, sa2, sa3, sb, swkv, sshift)
```

```python
import functools

import jax
import jax.numpy as jnp
from jax import lax
from jax.experimental import pallas as pl
from jax.experimental.pallas import tpu as pltpu

F32 = jnp.float32
BF16 = jnp.bfloat16

NORM_EPS = 1e-6
A_GROUPS = ((128, 1), (512, 4), (2048, 16))
A_HEADS = 8
A_HEAD_DIM = 128
A_WIDTH = A_HEADS * A_HEAD_DIM
A_BLK = 128
B_HEADS = 8
B_DIM = 128
B_CHUNK = 128
B_SUB = 16
C_HEAD_DIM = 64
C_GN_EPS = 64e-5
LANES = 128
MIB = 1024 * 1024


def _cparams(sem, vmem_mib):
    return pltpu.CompilerParams(dimension_semantics=sem, vmem_limit_bytes=vmem_mib * MIB)


def _rms(x, g):
    return x * lax.rsqrt(jnp.mean(x * x, axis=-1, keepdims=True) + NORM_EPS) * g


def _dot(a, b):
    return jnp.dot(a, b, preferred_element_type=F32)


def _dot_nt(a, b):
    return lax.dot_general(a, b, (((1,), (1,)), ((), ())), preferred_element_type=F32)


def _dot_tn(a, b):
    return lax.dot_general(a, b, (((0,), (0,)), ((), ())), preferred_element_type=F32)


def _sigmoid(x):
    return 1.0 / (1.0 + jnp.exp(-x))


def _silu(x):
    return x * _sigmoid(x)


def _norm_kernel(x_ref, g_ref, o_ref):
    o_ref[...] = _rms(x_ref[...], g_ref[...]).astype(o_ref.dtype)


def rmsnorm_rows(x, g):
    m, d = x.shape
    tm = min(m, 512)
    return pl.pallas_call(
        _norm_kernel,
        grid=(m // tm,),
        in_specs=[pl.BlockSpec((tm, d), lambda i: (i, 0)),
                  pl.BlockSpec((1, d), lambda i: (0, 0))],
        out_specs=pl.BlockSpec((tm, d), lambda i: (i, 0)),
        out_shape=jax.ShapeDtypeStruct((m, d), BF16),
        compiler_params=_cparams(("parallel",), 32),
        name="rmsnorm_rows",
    )(x, g.reshape(1, d))


def _mm_kernel(*refs, act, has_bias):
    if has_bias:
        a_ref, w_ref, b_ref, o_ref = refs
    else:
        a_ref, w_ref, o_ref = refs
    y = _dot(a_ref[...], w_ref[...])
    if has_bias:
        y = y + b_ref[...]
    if act == "tanh":
        y = jnp.tanh(y)
    elif act == "sigmoid":
        y = _sigmoid(y)
    o_ref[...] = y.astype(o_ref.dtype)


def matmul(a, w, bias=None, act=None, out_dtype=F32):
    m, k = a.shape
    n = w.shape[1]
    tm = min(m, 1024)
    tn = min(n, 1024)
    in_specs = [pl.BlockSpec((tm, k), lambda i, j: (i, 0)),
                pl.BlockSpec((k, tn), lambda i, j: (0, j))]
    args = [a, w]
    if bias is not None:
        in_specs.append(pl.BlockSpec((1, tn), lambda i, j: (0, j)))
        args.append(bias.reshape(1, n))
    return pl.pallas_call(
        functools.partial(_mm_kernel, act=act, has_bias=bias is not None),
        grid=(m // tm, n // tn),
        in_specs=in_specs,
        out_specs=pl.BlockSpec((tm, tn), lambda i, j: (i, j)),
        out_shape=jax.ShapeDtypeStruct((m, n), out_dtype),
        compiler_params=_cparams(("parallel", "parallel"), 40),
        name="matmul",
    )(*args)


def _mm_post_kernel(*refs, has_gate, emit_h):
    refs = list(refs)
    a_ref = refs.pop(0)
    gate_ref = refs.pop(0) if has_gate else None
    w_ref, g1_ref, xres_ref, gn_ref, x_out = refs[:5]
    a = a_ref[...]
    if has_gate:
        a = (a * gate_ref[...]).astype(BF16)
    y = _dot(a, w_ref[...])
    x_new = xres_ref[...] + _rms(y, g1_ref[...])
    x_out[...] = x_new
    if emit_h:
        refs[5][...] = _rms(x_new, gn_ref[...]).astype(BF16)


def matmul_post(a, w, g_post, xres, g_next, gate=None, emit_h=True):
    m, k = a.shape
    d = w.shape[1]
    tm = min(m, 256)
    row = lambda i: (i, 0)
    fixed = lambda i: (0, 0)
    in_specs = [pl.BlockSpec((tm, k), row)]
    args = [a]
    if gate is not None:
        in_specs.append(pl.BlockSpec((tm, k), row))
        args.append(gate)
    in_specs += [pl.BlockSpec((k, d), fixed), pl.BlockSpec((1, d), fixed),
                 pl.BlockSpec((tm, d), row), pl.BlockSpec((1, d), fixed)]
    args += [w, g_post.reshape(1, d), xres, g_next.reshape(1, d)]
    out_specs = [pl.BlockSpec((tm, d), row)]
    out_shape = [jax.ShapeDtypeStruct((m, d), F32)]
    if emit_h:
        out_specs.append(pl.BlockSpec((tm, d), row))
        out_shape.append(jax.ShapeDtypeStruct((m, d), BF16))
    res = pl.pallas_call(
        functools.partial(_mm_post_kernel, has_gate=gate is not None, emit_h=emit_h),
        grid=(m // tm,),
        in_specs=in_specs,
        out_specs=out_specs,
        out_shape=out_shape,
        compiler_params=_cparams(("parallel",), 48),
        name="matmul_post",
    )(*args)
    return (res[0], res[1]) if emit_h else (res[0], None)


def _ffn_kernel(h_ref, wg_ref, wu_ref, wd_ref, g1_ref, xres_ref, gn_ref, *outs, emit_h):
    if emit_h:
        x_out, h_out, acc_ref = outs
    else:
        x_out, acc_ref = outs
    j = pl.program_id(1)

    @pl.when(j == 0)
    def _():
        acc_ref[...] = jnp.zeros_like(acc_ref)

    h = h_ref[...]
    gate = _dot(h, wg_ref[...])
    up = _dot(h, wu_ref[...])
    act = (_silu(gate) * up).astype(BF16)
    acc_ref[...] += _dot(act, wd_ref[...])

    @pl.when(j == pl.num_programs(1) - 1)
    def _():
        x_new = xres_ref[...] + _rms(acc_ref[...], g1_ref[...])
        x_out[...] = x_new
        if emit_h:
            h_out[...] = _rms(x_new, gn_ref[...]).astype(BF16)


def ffn(h, w_up, w_down, g_post, xres, g_next, emit_h=True):
    m, d = h.shape
    f = w_down.shape[0]
    tm = min(m, 512)
    th = 512
    nj = f // th
    row = lambda i, j: (i, 0)
    fixed = lambda i, j: (0, 0)
    in_specs = [pl.BlockSpec((tm, d), row),
                pl.BlockSpec((d, th), lambda i, j: (0, j)),
                pl.BlockSpec((d, th), lambda i, j: (0, j + nj)),
                pl.BlockSpec((th, d), lambda i, j: (j, 0)),
                pl.BlockSpec((1, d), fixed),
                pl.BlockSpec((tm, d), row),
                pl.BlockSpec((1, d), fixed)]
    out_specs = [pl.BlockSpec((tm, d), row)]
    out_shape = [jax.ShapeDtypeStruct((m, d), F32)]
    if emit_h:
        out_specs.append(pl.BlockSpec((tm, d), row))
        out_shape.append(jax.ShapeDtypeStruct((m, d), BF16))
    res = pl.pallas_call(
        functools.partial(_ffn_kernel, emit_h=emit_h),
        grid=(m // tm, nj),
        in_specs=in_specs,
        out_specs=out_specs,
        out_shape=out_shape,
        scratch_shapes=[pltpu.VMEM((tm, d), F32)],
        compiler_params=_cparams(("parallel", "arbitrary"), 52),
        name="ffn",
    )(h, w_up, w_up, w_down, g_post.reshape(1, d), xres, g_next.reshape(1, d))
    return (res[0], res[1]) if emit_h else (res[0], None)


def _alibi_slope(shape):
    h = pl.program_id(1)
    expo = (h + 1).astype(F32) * (-8.0 / A_HEADS)
    return jnp.exp2(jnp.full(shape, 1.0, F32) * expo)


def _attn_prompt_kernel(q1, k1, v1, q2, k2, v2, q3, k3, v3, out_ref, o_scr, l_scr):
    t_len = out_ref.shape[0]
    blk = A_BLK
    scale = A_HEAD_DIM ** -0.5
    slope = _alibi_slope((blk, blk))
    qi = lax.broadcasted_iota(jnp.int32, (blk, blk), 0)
    ki = lax.broadcasted_iota(jnp.int32, (blk, blk), 1)
    dist_cur = (qi - ki).astype(F32)
    dist_prev = (qi - ki + blk).astype(F32)
    mask_cur = ki <= qi
    mask_prev = ki >= qi
    neg_inf = jnp.float32(-jnp.inf)

    def rows(start, dil):
        if dil == 1:
            return pl.ds(pl.multiple_of(start, blk), blk)
        return pl.ds(start, blk, stride=dil)

    def one_block(gi, q_ref, k_ref, v_ref, dil, start, prev_start, use_prev):
        q = q_ref[rows(start, dil), :].astype(BF16)
        kc = k_ref[rows(start, dil), :].astype(BF16)
        vc = v_ref[rows(start, dil), :].astype(BF16)
        s_c = _dot_nt(q, kc) * scale - slope * (dist_cur * float(dil))
        s_c = jnp.where(mask_cur, s_c, neg_inf)
        mx = jnp.max(s_c, axis=-1, keepdims=True)
        if use_prev is not None:
            kp = k_ref[rows(prev_start, dil), :].astype(BF16)
            vp = v_ref[rows(prev_start, dil), :].astype(BF16)
            s_p = _dot_nt(q, kp) * scale - slope * (dist_prev * float(dil))
            s_p = jnp.where(mask_prev, s_p + jnp.where(use_prev, 0.0, neg_inf), neg_inf)
            mx = jnp.maximum(mx, jnp.max(s_p, axis=-1, keepdims=True))
        p_c = jnp.exp(s_c - mx)
        den = jnp.sum(p_c, axis=-1, keepdims=True)
        acc = _dot(p_c.astype(BF16), vc)
        if use_prev is not None:
            p_p = jnp.exp(s_p - mx)
            den = den + jnp.sum(p_p, axis=-1, keepdims=True)
            acc = acc + _dot(p_p.astype(BF16), vp)
        o = acc / den
        lse = mx + jnp.log(den)
        o_scr[gi, rows(start, dil), :] = o
        l_scr[gi, rows(start, dil), :] = jnp.broadcast_to(lse, (blk, LANES))

    for gi, (q_ref, k_ref, v_ref) in enumerate(((q1, k1, v1), (q2, k2, v2), (q3, k3, v3))):
        window, dil = A_GROUPS[gi]
        assert window // dil == blk
        m_len = t_len // dil
        nb = m_len // blk
        assert nb * blk * dil == t_len

        def body(idx, carry, gi=gi, q_ref=q_ref, k_ref=k_ref, v_ref=v_ref, dil=dil, nb=nb):
            r = idx // nb
            n = idx - r * nb
            start = n * (blk * dil) + r
            if nb == 1:
                one_block(gi, q_ref, k_ref, v_ref, dil, start, None, None)
            else:
                prev_start = jnp.maximum(n - 1, 0) * (blk * dil) + r
                one_block(gi, q_ref, k_ref, v_ref, dil, start, prev_start, n > 0)
            return carry

        lax.fori_loop(0, dil * nb, body, 0)

    def merge(c, carry):
        sl = pl.ds(pl.multiple_of(c * blk, blk), blk)
        l0, l1, l2 = l_scr[0, sl, :], l_scr[1, sl, :], l_scr[2, sl, :]
        mx = jnp.maximum(jnp.maximum(l0, l1), l2)
        w0, w1, w2 = jnp.exp(l0 - mx), jnp.exp(l1 - mx), jnp.exp(l2 - mx)
        num = w0 * o_scr[0, sl, :] + w1 * o_scr[1, sl, :] + w2 * o_scr[2, sl, :]
        out_ref[sl, :] = (num / (w0 + w1 + w2)).astype(out_ref.dtype)
        return carry

    lax.fori_loop(0, t_len // blk, merge, 0)


def attn_prompt(proj):
    bn, t_len, _ = proj.shape
    hd = A_HEAD_DIM

    def col(cb):
        return pl.BlockSpec((None, t_len, hd), lambda b, h, cb=cb: (b, 0, cb * A_HEADS + h))

    in_specs = [col(3 * g + j) for g in range(3) for j in range(3)]
    return pl.pallas_call(
        _attn_prompt_kernel,
        grid=(bn, A_HEADS),
        in_specs=in_specs,
        out_specs=pl.BlockSpec((None, t_len, hd), lambda b, h: (b, 0, h)),
        out_shape=jax.ShapeDtypeStruct((bn, t_len, A_WIDTH), BF16),
        scratch_shapes=[pltpu.VMEM((3, t_len, hd), F32), pltpu.VMEM((3, t_len, LANES), F32)],
        compiler_params=_cparams(("parallel", "parallel"), 40),
        name="attn_prompt",
    )(*([proj] * 9))


def _attn_sample_kernel(q1, kn1, vn1, q2, kn2, vn2, q3, kn3, vn3,
                        kc1, vc1, kc2, vc2, kc3, vc3, out_ref):
    l_new = q1.shape[0]
    scale = A_HEAD_DIM ** -0.5
    neg_inf = jnp.float32(-jnp.inf)
    groups = ((q1, kn1, vn1, kc1, vc1), (q2, kn2, vn2, kc2, vc2), (q3, kn3, vn3, kc3, vc3))

    scores = []
    for gi, (q_ref, kn_ref, vn_ref, kc_ref, vc_ref) in enumerate(groups):
        window, dil = A_GROUPS[gi]
        n_buf = kc_ref.shape[0]
        q = q_ref[...].astype(BF16)
        for k_ref, v_ref, base in ((kc_ref, vc_ref, 0), (kn_ref, vn_ref, n_buf)):
            n_keys = k_ref.shape[0]
            li = lax.broadcasted_iota(jnp.int32, (l_new, n_keys), 0)
            pi = lax.broadcasted_iota(jnp.int32, (l_new, n_keys), 1) + base
            dist = n_buf + li - pi
            valid = jnp.logical_and(jnp.logical_and(dist >= 0, dist <= window),
                                    (dist & (dil - 1)) == 0)
            slope = _alibi_slope((l_new, n_keys))
            s = _dot_nt(q, k_ref[...].astype(BF16)) * scale - slope * dist.astype(F32)
            scores.append((jnp.where(valid, s, neg_inf), v_ref))

    mx = None
    for s, _ in scores:
        m = jnp.max(s, axis=-1, keepdims=True)
        mx = m if mx is None else jnp.maximum(mx, m)
    den = jnp.zeros((l_new, 1), F32)
    acc = jnp.zeros((l_new, A_HEAD_DIM), F32)
    for s, v_ref in scores:
        p = jnp.exp(s - mx)
        den = den + jnp.sum(p, axis=-1, keepdims=True)
        acc = acc + _dot(p.astype(BF16), v_ref[...].astype(BF16))
    out_ref[...] = (acc / den).astype(out_ref.dtype)


def attn_sample(proj, caches, li):
    bn, l_new, _ = proj.shape
    hd = A_HEAD_DIM

    def col(cb):
        return pl.BlockSpec((None, l_new, hd), lambda b, h, cb=cb: (b, 0, cb * A_HEADS + h))

    def cache_spec(n_buf, kv):
        return pl.BlockSpec((None, None, n_buf, hd),
                            lambda b, h, kv=kv: (li, b, 0, kv * A_HEADS + h))

    in_specs = [col(3 * g + j) for g in range(3) for j in range(3)]
    args = [proj] * 9
    for g in range(3):
        n_buf = caches[g].shape[2]
        in_specs += [cache_spec(n_buf, 0), cache_spec(n_buf, 1)]
        args += [caches[g], caches[g]]
    return pl.pallas_call(
        _attn_sample_kernel,
        grid=(bn, A_HEADS),
        in_specs=in_specs,
        out_specs=pl.BlockSpec((None, l_new, hd), lambda b, h: (b, 0, h)),
        out_shape=jax.ShapeDtypeStruct((bn, l_new, A_WIDTH), F32),
        compiler_params=_cparams(("parallel", "parallel"), 32),
        name="attn_sample",
    )(*args)


def _gla_kernel(bq_ref, bf_ref, bi_ref, bg_ref, lbp_ref, gain_ref, s0_ref,
                o_ref, sfin_ref, st_scr, att_scr, *, li, cin):
    c = pl.program_id(2)
    ch = B_CHUNK

    @pl.when(c == 0)
    def _():
        st_scr[...] = s0_ref[...].T

    lbp = lbp_ref[...]
    e = jnp.exp(lbp - jnp.max(lbp, axis=0, keepdims=True))
    soft = e / jnp.sum(e, axis=0, keepdims=True)
    lb = jnp.sum(soft[0:li + 1], axis=0, keepdims=True) - soft[0:1]

    def pad(x):
        if cin == ch:
            return x
        return jnp.concatenate([x, jnp.zeros((ch - cin, x.shape[1]), x.dtype)], axis=0)

    row = lax.broadcasted_iota(jnp.int32, (ch, B_DIM), 0)
    lane = lax.broadcasted_iota(jnp.int32, (B_SUB, ch), 1)
    row_sub = lax.broadcasted_iota(jnp.int32, (B_SUB, B_DIM), 0)
    live = row < cin

    q = _silu(pad(bq_ref[...]))
    fgate = lb + (1.0 - lb) * _sigmoid(pad(bf_ref[...]))
    g = jnp.where(live, jnp.log(fgate), 0.0)
    kk = jnp.where(live, 1.0 - fgate, 0.0)
    v = pad(bi_ref[...])
    v16 = v.astype(BF16)

    tri = (row >= lax.broadcasted_iota(jnp.int32, (ch, ch), 1)).astype(F32)
    b = jnp.dot(tri, g, preferred_element_type=F32, precision=lax.Precision.HIGHEST)
    b_last = b[ch - 1:ch]
    st = st_scr[...]

    o = _dot_nt((q * jnp.exp(b)).astype(BF16), st.astype(BF16))

    for qb in range(ch // B_SUB):
        r0 = qb * B_SUB
        q_s = q[r0:r0 + B_SUB]
        b_s = b[r0:r0 + B_SUB]
        kk_s = kk[r0:r0 + B_SUB]
        if qb > 0:
            b_ref = b[r0 - 1:r0]
            qf = q_s * jnp.exp(b_s - b_ref)
            earlier = row < r0
            kf = jnp.where(earlier, kk * jnp.exp(jnp.where(earlier, b_ref - b, 0.0)), 0.0)
            att = _dot_nt(qf.astype(BF16), kf.astype(BF16))
        else:
            att = jnp.zeros((B_SUB, ch), F32)
        for s in range(B_SUB):
            causal = row_sub >= s
            dec = jnp.where(causal, jnp.exp(jnp.where(causal, b_s - b_s[s:s + 1], 0.0)), 0.0)
            colv = jnp.sum(q_s * dec * kk_s[s:s + 1], axis=-1, keepdims=True)
            att = jnp.where(lane == r0 + s, colv, att)
        att_scr[r0:r0 + B_SUB, :] = att

    o = o + _dot(att_scr[...].astype(BF16), v16)

    kd = kk * jnp.exp(b_last - b)
    st_new = st * jnp.exp(b_last) + _dot_tn(v16, kd.astype(BF16))
    st_scr[...] = st_new

    on = _rms(o, gain_ref[...]) * _silu(pad(bg_ref[...]))
    o_ref[...] = on[0:cin].astype(o_ref.dtype)

    @pl.when(c == pl.num_programs(2) - 1)
    def _():
        sfin_ref[...] = st_new.T


def gla(proj, lb_params, b_gain, s0, li):
    bn, t_len, ab_in = proj.shape
    n_ab = lb_params.shape[0]
    cin = min(B_CHUNK, t_len)
    nc = t_len // cin
    base = ab_in // B_DIM - 4 * B_HEADS

    def col(j):
        return pl.BlockSpec((None, cin, B_DIM), lambda b, h, c, j=j: (b, c, base + j * B_HEADS + h))

    out_dtype = F32 if cin < 16 else BF16
    return pl.pallas_call(
        functools.partial(_gla_kernel, li=li, cin=cin),
        grid=(bn, B_HEADS, nc),
        in_specs=[col(0), col(1), col(2), col(3),
                  pl.BlockSpec((n_ab, B_DIM), lambda b, h, c: (0, h)),
                  pl.BlockSpec((None, 1, B_DIM), lambda b, h, c: (li, 0, h)),
                  pl.BlockSpec((None, None, B_DIM, B_DIM), lambda b, h, c: (b, h, 0, 0))],
        out_specs=[pl.BlockSpec((None, cin, B_DIM), lambda b, h, c: (b, c, h)),
                   pl.BlockSpec((None, None, B_DIM, B_DIM), lambda b, h, c: (b, h, 0, 0))],
        out_shape=[jax.ShapeDtypeStruct((bn, t_len, B_HEADS * B_DIM), out_dtype),
                   jax.ShapeDtypeStruct((bn, B_HEADS, B_DIM, B_DIM), F32)],
        scratch_shapes=[pltpu.VMEM((B_DIM, B_DIM), F32), pltpu.VMEM((B_CHUNK, B_CHUNK), F32)],
        compiler_params=_cparams(("parallel", "parallel", "arbitrary"), 32),
        name="gla",
    )(proj, proj, proj, proj, lb_params, b_gain.reshape(n_ab, 1, -1), s0)


def _mix_kernel(x_ref, xp_ref, sh_ref, g_ref, mu_ref, xm_ref, hl_ref):
    i = pl.program_id(1)
    tt = x_ref.shape[0]
    g = g_ref[...]
    h = _rms(x_ref[...], g)
    hp = _rms(xp_ref[...], g)
    first = jnp.where(i == 0, sh_ref[...], hp[7:8])
    rolled = pltpu.roll(h, 1, axis=0)
    row = lax.broadcasted_iota(jnp.int32, h.shape, 0)
    prev = jnp.where(row == 0, first, rolled)
    xx = prev - h
    for j in range(6):
        xm_ref[j] = (h + xx * mu_ref[j:j + 1]).astype(BF16)

    @pl.when(i == pl.num_programs(1) - 1)
    def _():
        hl_ref[...] = h[tt - 1:tt]


def rwkv_mix(x, shift, g, mu):
    bn, t_len, d = x.shape
    tt = min(t_len, 256)
    sub = tt // 8
    xm, hl = pl.pallas_call(
        _mix_kernel,
        grid=(bn, t_len // tt),
        in_specs=[pl.BlockSpec((None, tt, d), lambda b, i: (b, i, 0)),
                  pl.BlockSpec((None, 8, d), lambda b, i: (b, jnp.maximum(i * sub - 1, 0), 0)),
                  pl.BlockSpec((None, 1, d), lambda b, i: (b, 0, 0)),
                  pl.BlockSpec((1, d), lambda b, i: (0, 0)),
                  pl.BlockSpec((6, d), lambda b, i: (0, 0))],
        out_specs=[pl.BlockSpec((6, None, tt, d), lambda b, i: (0, b, i, 0)),
                   pl.BlockSpec((None, 1, d), lambda b, i: (b, 0, 0))],
        out_shape=[jax.ShapeDtypeStruct((6, bn, t_len, d), BF16),
                   jax.ShapeDtypeStruct((bn, 1, d), F32)],
        compiler_params=_cparams(("parallel", "arbitrary"), 40),
        name="rwkv_mix",
    )(x, x, shift.reshape(bn, 1, d), g.reshape(1, d), mu)
    return xm, hl.reshape(bn, d)


def _scan_kernel(r_ref, k_ref, v_ref, wp_ref, ap_ref, par_ref, s0_ref,
                 o_ref, sfin_ref, st, orow, *, tc):
    c = pl.program_id(1)
    n = C_HEAD_DIM

    @pl.when(c == 0)
    def _():
        st[...] = s0_ref[...]

    def step(t, carry):
        c_kk, c_ka, c_rk, ln_w, ln_b = (par_ref[j] for j in range(5))
        z = -wp_ref[t]
        softplus = jnp.maximum(z, 0.0) + jnp.log(1.0 + jnp.exp(-jnp.abs(z)))
        w = jnp.exp(-jnp.exp(-softplus - 0.5))
        a = _sigmoid(ap_ref[t])
        kt = k_ref[t]
        kk = kt * c_kk
        nrm = jnp.sqrt(jnp.sum(kk * kk, axis=0, keepdims=True))
        kk = kk / jnp.maximum(nrm, 1e-12)
        k2 = kt * (1.0 + (a - 1.0) * c_ka)
        kb = kk * a
        nkk = -kk
        rt = r_ref[t]
        for vi in range(n):
            s_v = st[vi]
            sa = jnp.sum(s_v * nkk, axis=0, keepdims=True)
            s_n = s_v * w + sa * kb + v_ref[t, pl.ds(vi, 1), :] * k2
            st[vi] = s_n
            orow[pl.ds(vi, 1), :] = jnp.sum(s_n * rt, axis=0, keepdims=True)
        o = orow[...]
        mean = jnp.mean(o, axis=0, keepdims=True)
        var = jnp.mean(jnp.square(o - mean), axis=0, keepdims=True)
        o = (o - mean) * lax.rsqrt(var + C_GN_EPS) * ln_w + ln_b
        bonus = jnp.sum(rt * k2 * c_rk, axis=0, keepdims=True)
        o_ref[t] = o + bonus * v_ref[t]
        return carry

    lax.fori_loop(0, tc, step, 0)

    @pl.when(c == pl.num_programs(1) - 1)
    def _():
        sfin_ref[...] = st[...]


def rwkv_scan(r, k, v, wp, ap, par, s0):
    t_len, n, bh = r.shape
    tc = min(t_len, 32)
    seq = pl.BlockSpec((tc, n, LANES), lambda l, c: (c, 0, l))
    return pl.pallas_call(
        functools.partial(_scan_kernel, tc=tc),
        grid=(bh // LANES, t_len // tc),
        in_specs=[seq] * 5 + [pl.BlockSpec((5, n, LANES), lambda l, c: (0, 0, l)),
                              pl.BlockSpec((n, n, LANES), lambda l, c: (0, 0, l))],
        out_specs=[seq, pl.BlockSpec((n, n, LANES), lambda l, c: (0, 0, l))],
        out_shape=[jax.ShapeDtypeStruct((t_len, n, bh), F32),
                   jax.ShapeDtypeStruct((n, n, bh), F32)],
        scratch_shapes=[pltpu.VMEM((n, n, LANES), F32), pltpu.VMEM((n, LANES), F32)],
        compiler_params=_cparams(("parallel", "arbitrary"), 40),
        name="rwkv_scan",
    )(r, k, v, wp, ap, par, s0)


def _pad_to(x, axis, mult):
    pad = (-x.shape[axis]) % mult
    if pad == 0:
        return x
    widths = [(0, 0)] * x.ndim
    widths[axis] = (0, pad)
    return jnp.pad(x, widths)


def _ab_layer(x, h, li, p, caches, b_state):
    bn, t_len, d = x.shape
    proj = matmul(h, p["w_in_ab"][li]).reshape(bn, t_len, -1)
    kv_new = []
    for g in range(3):
        kv = proj[:, :, (3 * g + 1) * A_WIDTH:(3 * g + 3) * A_WIDTH]
        kv_new.append(kv.reshape(bn, t_len, 2, A_HEADS, A_HEAD_DIM))
    if caches is None:
        a_out = attn_prompt(proj)
        new_bufs = [kv_new[g][:, -min(A_GROUPS[g][0], t_len):] for g in range(3)]
        s0 = jnp.zeros((bn, B_HEADS, B_DIM, B_DIM), F32)
    else:
        flat = [c.reshape(c.shape[0], c.shape[1], c.shape[2], -1) for c in caches]
        a_out = attn_sample(proj, flat, li).astype(BF16)
        new_bufs = [jnp.concatenate([caches[g][li], kv_new[g]], axis=1)[:, t_len:] for g in range(3)]
        s0 = b_state[li]
    b_out, s_b = gla(proj, p["b_lower_bounds"], p["b_norm_gain"], s0, li)
    mix_in = jnp.concatenate([a_out, b_out.astype(BF16)], axis=-1).reshape(bn * t_len, -1)
    return mix_in, new_bufs, s_b


def _to_scan(t, bn, t_len):
    hh = t.shape[-1] // C_HEAD_DIM
    t = t.reshape(bn, t_len, hh, C_HEAD_DIM)
    return jnp.transpose(t, (1, 3, 0, 2)).reshape(t_len, C_HEAD_DIM, bn * hh)


def _rwkv_layer(x, li, g_pre, p, shift, wkv):
    bn, t_len, d = x.shape
    m = bn * t_len
    hh = d // C_HEAD_DIM
    xm, h_last = rwkv_mix(x, shift, g_pre, p["c_mu"][li])
    xm = xm.reshape(6, m, d)
    w_rkv = p["c_w_rkv"][li]
    r = matmul(xm[0], w_rkv[0])
    k = matmul(xm[2], w_rkv[1])
    v = matmul(xm[3], w_rkv[2])
    w_lo = matmul(xm[1], _pad_to(p["c_w1"][li], 1, LANES), act="tanh", out_dtype=BF16)
    wp = matmul(w_lo, _pad_to(p["c_w2"][li], 0, LANES), bias=p["c_w0"][li])
    a_lo = matmul(xm[4], _pad_to(p["c_a1"][li], 1, LANES), out_dtype=BF16)
    ap = matmul(a_lo, _pad_to(p["c_a2"][li], 0, LANES), bias=p["c_a0"][li])
    g_lo = matmul(xm[5], _pad_to(p["c_g1"][li], 1, LANES), act="sigmoid", out_dtype=BF16)
    gate = matmul(g_lo, _pad_to(p["c_g2"][li], 0, LANES))

    def lanes(vec):
        return jnp.tile(vec.reshape(hh, C_HEAD_DIM).T, (1, bn))

    par = jnp.stack([lanes(p["c_k_k"][li]), lanes(p["c_k_a"][li]), lanes(p["c_r_k"][li].reshape(-1)),
                     lanes(p["c_ln_w"][li]), lanes(p["c_ln_b"][li])])
    s0 = jnp.transpose(wkv, (2, 3, 0, 1)).reshape(C_HEAD_DIM, C_HEAD_DIM, bn * hh)
    o, s_fin = rwkv_scan(*(_to_scan(t, bn, t_len) for t in (r, k, v, wp, ap)), par, s0)
    o = jnp.transpose(o.reshape(t_len, C_HEAD_DIM, bn, hh), (2, 0, 3, 1)).reshape(m, d)
    s_fin = jnp.transpose(s_fin.reshape(C_HEAD_DIM, C_HEAD_DIM, bn, hh), (2, 3, 0, 1))
    return o, gate, s_fin, h_last


def _trunk(x, caches, b_states, c_wkv, c_shift, p):
    bn, t_len, d = x.shape
    m = bn * t_len
    depth = p["norm_gains"].shape[0]
    new_a = ([], [], [])
    new_b, new_wkv, new_shift = [], [], []
    h = rmsnorm_rows(x.reshape(m, d), p["norm_gains"][0, 0])
    for layer in range(depth):
        gains = p["norm_gains"][layer]
        li = layer // 2
        last = layer == depth - 1
        next_pre = p["norm_gains"][layer + 1, 0] if not last else gains[0]
        if layer % 2 == 0:
            mix_in, kv_bufs, s_b = _ab_layer(x, h, li, p, caches, b_states)
            for lst, kv in zip(new_a, kv_bufs):
                lst.append(kv)
            new_b.append(s_b)
            x2, h2 = matmul_post(mix_in, p["w_out_ab"][li], gains[1], x.reshape(m, d), gains[2])
        else:
            wkv0 = jnp.zeros((bn, d // C_HEAD_DIM, C_HEAD_DIM, C_HEAD_DIM), F32) if c_wkv is None else c_wkv[li]
            sh0 = jnp.zeros((bn, d), F32) if c_shift is None else c_shift[li]
            o, gate, s_wkv, s_shift = _rwkv_layer(x, li, gains[0], p, sh0, wkv0)
            new_wkv.append(s_wkv)
            new_shift.append(s_shift)
            x2, h2 = matmul_post(o, p["c_w_out"][li], gains[1], x.reshape(m, d), gains[2], gate=gate)
        next_even = (layer + 1) % 2 == 0 and not last
        x3, h = ffn(h2, p["w_ffn_up"][layer], p["w_ffn_down"][layer], gains[3], x2, next_pre,
                    emit_h=next_even)
        x = x3.reshape(bn, t_len, d)
    return (x, tuple(jnp.stack(lst) for lst in new_a), jnp.stack(new_b),
            jnp.stack(new_wkv), jnp.stack(new_shift))


def kernel(x_prompt, x_sample, cache_a1_kv, cache_a2_kv, cache_a3_kv, state_b, state_c_wkv, state_c_shift, norm_gains, w_in_ab, w_out_ab, b_lower_bounds, b_norm_gain, c_mu, c_w_rkv, c_w0, c_w1, c_w2, c_a0, c_a1, c_a2, c_g1, c_g2, c_k_k, c_k_a, c_r_k, c_ln_w, c_ln_b, c_w_out, w_ffn_up, w_ffn_down):
    to16 = lambda w: w.astype(BF16)
    p = dict(norm_gains=norm_gains, w_in_ab=to16(w_in_ab), w_out_ab=to16(w_out_ab),
             b_lower_bounds=b_lower_bounds, b_norm_gain=b_norm_gain, c_mu=c_mu,
             c_w_rkv=to16(c_w_rkv), c_w0=c_w0, c_w1=to16(c_w1), c_w2=to16(c_w2),
             c_a0=c_a0, c_a1=to16(c_a1), c_a2=to16(c_a2), c_g1=to16(c_g1), c_g2=to16(c_g2),
             c_k_k=c_k_k, c_k_a=c_k_a, c_r_k=c_r_k, c_ln_w=c_ln_w, c_ln_b=c_ln_b,
             c_w_out=to16(c_w_out), w_ffn_up=to16(w_ffn_up), w_ffn_down=to16(w_ffn_down))
    y_p, (pa1, pa2, pa3), pb, pwkv, pshift = _trunk(x_prompt, None, None, None, None, p)
    y_s, (sa1, sa2, sa3), sb, swkv, sshift = _trunk(
        x_sample, (cache_a1_kv, cache_a2_kv, cache_a3_kv), state_b, state_c_wkv, state_c_shift, p)
    return (y_p, y_s, pa1, pa2, pa3, pb, pwkv, pshift, sa1, sa2, sa3, sb, swkv, sshift)
```

```python
import functools

import jax
import jax.numpy as jnp
from jax import lax
from jax.experimental import pallas as pl
from jax.experimental.pallas import tpu as pltpu

F32 = jnp.float32
BF16 = jnp.bfloat16

NORM_EPS = 1e-6
A_GROUPS = ((128, 1), (512, 4), (2048, 16))
A_HEADS = 8
A_HEAD_DIM = 128
A_WIDTH = A_HEADS * A_HEAD_DIM
A_BLK = 128
B_HEADS = 8
B_DIM = 128
B_CHUNK = 128
B_SUB = 8
B_HEADS_PER_STEP = 2
C_HEAD_DIM = 64
C_GN_EPS = 64e-5
LANES = 128
MIB = 1024 * 1024


def _cparams(sem, vmem_mib):
    return pltpu.CompilerParams(dimension_semantics=sem, vmem_limit_bytes=vmem_mib * MIB)


def _rms(x, g):
    return x * lax.rsqrt(jnp.mean(x * x, axis=-1, keepdims=True) + NORM_EPS) * g


def _dot(a, b):
    return jnp.dot(a, b, preferred_element_type=F32)


def _dot_nt(a, b):
    return lax.dot_general(a, b, (((1,), (1,)), ((), ())), preferred_element_type=F32)


def _dot_tn(a, b):
    return lax.dot_general(a, b, (((0,), (0,)), ((), ())), preferred_element_type=F32)


def _sigmoid(x):
    return 1.0 / (1.0 + jnp.exp(-x))


def _silu(x):
    return x * _sigmoid(x)


def _norm_kernel(x_ref, g_ref, o_ref):
    o_ref[...] = _rms(x_ref[...], g_ref[...]).astype(o_ref.dtype)


def rmsnorm_rows(x, g):
    m, d = x.shape
    tm = min(m, 512)
    return pl.pallas_call(
        _norm_kernel,
        grid=(m // tm,),
        in_specs=[pl.BlockSpec((tm, d), lambda i: (i, 0)),
                  pl.BlockSpec((1, d), lambda i: (0, 0))],
        out_specs=pl.BlockSpec((tm, d), lambda i: (i, 0)),
        out_shape=jax.ShapeDtypeStruct((m, d), BF16),
        compiler_params=_cparams(("parallel",), 32),
        name="rmsnorm_rows",
    )(x, g.reshape(1, d))


def _mm_kernel(*refs, act, has_bias):
    if has_bias:
        a_ref, w_ref, b_ref, o_ref = refs
    else:
        a_ref, w_ref, o_ref = refs
    y = _dot(a_ref[...], w_ref[...])
    if has_bias:
        y = y + b_ref[...]
    if act == "tanh":
        y = jnp.tanh(y)
    elif act == "sigmoid":
        y = _sigmoid(y)
    o_ref[...] = y.astype(o_ref.dtype)


def matmul(a, w, bias=None, act=None, out_dtype=F32, tb=None):
    m, k = a.shape
    n = w.shape[1]
    tn = min(n, 1024)
    nj = n // tn
    if tb is None:
        tm = min(m, 1024)
        out_shape = (m, n)
        out_index = lambda i, j: (i, j)
    else:
        bn, t_len = tb
        tm = min(t_len, 1024)
        nt = t_len // tm
        out_shape = (t_len, bn * n)
        out_index = lambda i, j: (i % nt, (i // nt) * nj + j)
    in_specs = [pl.BlockSpec((tm, k), lambda i, j: (i, 0)),
                pl.BlockSpec((k, tn), lambda i, j: (0, j))]
    args = [a, w]
    if bias is not None:
        in_specs.append(pl.BlockSpec((1, tn), lambda i, j: (0, j)))
        args.append(bias.reshape(1, n))
    return pl.pallas_call(
        functools.partial(_mm_kernel, act=act, has_bias=bias is not None),
        grid=(m // tm, nj),
        in_specs=in_specs,
        out_specs=pl.BlockSpec((tm, tn), out_index),
        out_shape=jax.ShapeDtypeStruct(out_shape, out_dtype),
        compiler_params=_cparams(("parallel", "parallel"), 40),
        name="matmul",
    )(*args)


def _mm_post_kernel(*refs, has_gate, emit_h):
    refs = list(refs)
    a_ref = refs.pop(0)
    gate_ref = refs.pop(0) if has_gate else None
    w_ref, g1_ref, xres_ref, gn_ref, x_out = refs[:5]
    a = a_ref[...]
    if has_gate:
        a = (a * gate_ref[...]).astype(BF16)
    y = _dot(a, w_ref[...])
    x_new = xres_ref[...] + _rms(y, g1_ref[...])
    x_out[...] = x_new
    if emit_h:
        refs[5][...] = _rms(x_new, gn_ref[...]).astype(BF16)


def matmul_post(a, w, g_post, xres, g_next, gate=None, emit_h=True, tb=None):
    m, d = xres.shape
    k = w.shape[0]
    tm = min(m, 256) if tb is None else min(tb[1], 256)
    row = lambda i: (i, 0)
    fixed = lambda i: (0, 0)
    if tb is None:
        a_row = row
    else:
        nt = tb[1] // tm
        a_row = lambda i: (i % nt, i // nt)
    in_specs = [pl.BlockSpec((tm, k), a_row)]
    args = [a]
    if gate is not None:
        in_specs.append(pl.BlockSpec((tm, k), a_row))
        args.append(gate)
    in_specs += [pl.BlockSpec((k, d), fixed), pl.BlockSpec((1, d), fixed),
                 pl.BlockSpec((tm, d), row), pl.BlockSpec((1, d), fixed)]
    args += [w, g_post.reshape(1, d), xres, g_next.reshape(1, d)]
    out_specs = [pl.BlockSpec((tm, d), row)]
    out_shape = [jax.ShapeDtypeStruct((m, d), F32)]
    if emit_h:
        out_specs.append(pl.BlockSpec((tm, d), row))
        out_shape.append(jax.ShapeDtypeStruct((m, d), BF16))
    res = pl.pallas_call(
        functools.partial(_mm_post_kernel, has_gate=gate is not None, emit_h=emit_h),
        grid=(m // tm,),
        in_specs=in_specs,
        out_specs=out_specs,
        out_shape=out_shape,
        compiler_params=_cparams(("parallel",), 48),
        name="matmul_post",
    )(*args)
    return (res[0], res[1]) if emit_h else (res[0], None)


def _ffn_kernel(h_ref, wg_ref, wu_ref, wd_ref, g1_ref, xres_ref, gn_ref, *outs, emit_h):
    if emit_h:
        x_out, h_out, acc_ref = outs
    else:
        x_out, acc_ref = outs
    j = pl.program_id(1)

    @pl.when(j == 0)
    def _():
        acc_ref[...] = jnp.zeros_like(acc_ref)

    h = h_ref[...]
    gate = _dot(h, wg_ref[...])
    up = _dot(h, wu_ref[...])
    act = (_silu(gate) * up).astype(BF16)
    acc_ref[...] += _dot(act, wd_ref[...])

    @pl.when(j == pl.num_programs(1) - 1)
    def _():
        x_new = xres_ref[...] + _rms(acc_ref[...], g1_ref[...])
        x_out[...] = x_new
        if emit_h:
            h_out[...] = _rms(x_new, gn_ref[...]).astype(BF16)


def ffn(h, w_up, w_down, g_post, xres, g_next, emit_h=True):
    m, d = h.shape
    f = w_down.shape[0]
    tm = min(m, 512)
    th = 512
    nj = f // th
    row = lambda i, j: (i, 0)
    fixed = lambda i, j: (0, 0)
    in_specs = [pl.BlockSpec((tm, d), row),
                pl.BlockSpec((d, th), lambda i, j: (0, j)),
                pl.BlockSpec((d, th), lambda i, j: (0, j + nj)),
                pl.BlockSpec((th, d), lambda i, j: (j, 0)),
                pl.BlockSpec((1, d), fixed),
                pl.BlockSpec((tm, d), row),
                pl.BlockSpec((1, d), fixed)]
    out_specs = [pl.BlockSpec((tm, d), row)]
    out_shape = [jax.ShapeDtypeStruct((m, d), F32)]
    if emit_h:
        out_specs.append(pl.BlockSpec((tm, d), row))
        out_shape.append(jax.ShapeDtypeStruct((m, d), BF16))
    res = pl.pallas_call(
        functools.partial(_ffn_kernel, emit_h=emit_h),
        grid=(m // tm, nj),
        in_specs=in_specs,
        out_specs=out_specs,
        out_shape=out_shape,
        scratch_shapes=[pltpu.VMEM((tm, d), F32)],
        compiler_params=_cparams(("parallel", "arbitrary"), 52),
        name="ffn",
    )(h, w_up, w_up, w_down, g_post.reshape(1, d), xres, g_next.reshape(1, d))
    return (res[0], res[1]) if emit_h else (res[0], None)


def _alibi_slope(shape):
    h = pl.program_id(1)
    expo = (h + 1).astype(F32) * (-8.0 / A_HEADS)
    return jnp.exp2(jnp.full(shape, 1.0, F32) * expo)


def _attn_prompt_kernel(q1, k1, v1, q2, k2, v2, q3, k3, v3, out_ref, o_scr, l_scr):
    t_len = out_ref.shape[0]
    blk = A_BLK
    scale = A_HEAD_DIM ** -0.5
    slope = _alibi_slope((blk, blk))
    qi = lax.broadcasted_iota(jnp.int32, (blk, blk), 0)
    ki = lax.broadcasted_iota(jnp.int32, (blk, blk), 1)
    dist_cur = (qi - ki).astype(F32)
    dist_prev = (qi - ki + blk).astype(F32)
    mask_cur = ki <= qi
    mask_prev = ki >= qi
    neg_inf = jnp.float32(-jnp.inf)

    def rows(start, dil):
        if dil == 1:
            return pl.ds(pl.multiple_of(start, blk), blk)
        return pl.ds(start, blk, stride=dil)

    def one_block(gi, q_ref, k_ref, v_ref, dil, start, prev_start, use_prev):
        q = q_ref[rows(start, dil), :].astype(BF16)
        kc = k_ref[rows(start, dil), :].astype(BF16)
        vc = v_ref[rows(start, dil), :].astype(BF16)
        s_c = _dot_nt(q, kc) * scale - slope * (dist_cur * float(dil))
        s_c = jnp.where(mask_cur, s_c, neg_inf)
        mx = jnp.max(s_c, axis=-1, keepdims=True)
        if use_prev is not None:
            kp = k_ref[rows(prev_start, dil), :].astype(BF16)
            vp = v_ref[rows(prev_start, dil), :].astype(BF16)
            s_p = _dot_nt(q, kp) * scale - slope * (dist_prev * float(dil))
            s_p = jnp.where(mask_prev, s_p + jnp.where(use_prev, 0.0, neg_inf), neg_inf)
            mx = jnp.maximum(mx, jnp.max(s_p, axis=-1, keepdims=True))
        p_c = jnp.exp(s_c - mx)
        den = jnp.sum(p_c, axis=-1, keepdims=True)
        acc = _dot(p_c.astype(BF16), vc)
        if use_prev is not None:
            p_p = jnp.exp(s_p - mx)
            den = den + jnp.sum(p_p, axis=-1, keepdims=True)
            acc = acc + _dot(p_p.astype(BF16), vp)
        o = acc / den
        lse = mx + jnp.log(den)
        o_scr[gi, rows(start, dil), :] = o
        l_scr[gi, rows(start, dil), :] = jnp.broadcast_to(lse, (blk, LANES))

    for gi, (q_ref, k_ref, v_ref) in enumerate(((q1, k1, v1), (q2, k2, v2), (q3, k3, v3))):
        window, dil = A_GROUPS[gi]
        assert window // dil == blk
        m_len = t_len // dil
        nb = m_len // blk
        assert nb * blk * dil == t_len

        def body(idx, carry, gi=gi, q_ref=q_ref, k_ref=k_ref, v_ref=v_ref, dil=dil, nb=nb):
            r = idx // nb
            n = idx - r * nb
            start = n * (blk * dil) + r
            if nb == 1:
                one_block(gi, q_ref, k_ref, v_ref, dil, start, None, None)
            else:
                prev_start = jnp.maximum(n - 1, 0) * (blk * dil) + r
                one_block(gi, q_ref, k_ref, v_ref, dil, start, prev_start, n > 0)
            return carry

        lax.fori_loop(0, dil * nb, body, 0, unroll=8)

    def merge(c, carry):
        sl = pl.ds(pl.multiple_of(c * blk, blk), blk)
        l0, l1, l2 = l_scr[0, sl, :], l_scr[1, sl, :], l_scr[2, sl, :]
        mx = jnp.maximum(jnp.maximum(l0, l1), l2)
        w0, w1, w2 = jnp.exp(l0 - mx), jnp.exp(l1 - mx), jnp.exp(l2 - mx)
        num = w0 * o_scr[0, sl, :] + w1 * o_scr[1, sl, :] + w2 * o_scr[2, sl, :]
        out_ref[sl, :] = (num / (w0 + w1 + w2)).astype(out_ref.dtype)
        return carry

    lax.fori_loop(0, t_len // blk, merge, 0, unroll=2)


def attn_prompt(proj):
    bn, t_len, _ = proj.shape
    hd = A_HEAD_DIM

    def col(cb):
        return pl.BlockSpec((None, t_len, hd), lambda b, h, cb=cb: (b, 0, cb * A_HEADS + h))

    in_specs = [col(3 * g + j) for g in range(3) for j in range(3)]
    return pl.pallas_call(
        _attn_prompt_kernel,
        grid=(bn, A_HEADS),
        in_specs=in_specs,
        out_specs=pl.BlockSpec((None, t_len, hd), lambda b, h: (b, 0, h)),
        out_shape=jax.ShapeDtypeStruct((bn, t_len, A_WIDTH), BF16),
        scratch_shapes=[pltpu.VMEM((3, t_len, hd), F32), pltpu.VMEM((3, t_len, LANES), F32)],
        compiler_params=_cparams(("parallel", "parallel"), 40),
        name="attn_prompt",
    )(*([proj] * 9))


def _attn_sample_kernel(q1, kn1, vn1, q2, kn2, vn2, q3, kn3, vn3,
                        kc1, vc1, kc2, vc2, kc3, vc3, out_ref):
    l_new = q1.shape[0]
    scale = A_HEAD_DIM ** -0.5
    neg_inf = jnp.float32(-jnp.inf)
    groups = ((q1, kn1, vn1, kc1, vc1), (q2, kn2, vn2, kc2, vc2), (q3, kn3, vn3, kc3, vc3))

    scores = []
    for gi, (q_ref, kn_ref, vn_ref, kc_ref, vc_ref) in enumerate(groups):
        window, dil = A_GROUPS[gi]
        n_buf = kc_ref.shape[0]
        q = q_ref[...].astype(BF16)
        for k_ref, v_ref, base in ((kc_ref, vc_ref, 0), (kn_ref, vn_ref, n_buf)):
            n_keys = k_ref.shape[0]
            li = lax.broadcasted_iota(jnp.int32, (l_new, n_keys), 0)
            pi = lax.broadcasted_iota(jnp.int32, (l_new, n_keys), 1) + base
            dist = n_buf + li - pi
            valid = jnp.logical_and(jnp.logical_and(dist >= 0, dist <= window),
                                    (dist & (dil - 1)) == 0)
            slope = _alibi_slope((l_new, n_keys))
            s = _dot_nt(q, k_ref[...].astype(BF16)) * scale - slope * dist.astype(F32)
            scores.append((jnp.where(valid, s, neg_inf), v_ref))

    mx = None
    for s, _ in scores:
        m = jnp.max(s, axis=-1, keepdims=True)
        mx = m if mx is None else jnp.maximum(mx, m)
    den = jnp.zeros((l_new, 1), F32)
    acc = jnp.zeros((l_new, A_HEAD_DIM), F32)
    for s, v_ref in scores:
        p = jnp.exp(s - mx)
        den = den + jnp.sum(p, axis=-1, keepdims=True)
        acc = acc + _dot(p.astype(BF16), v_ref[...].astype(BF16))
    out_ref[...] = (acc / den).astype(out_ref.dtype)


def attn_sample(proj, caches, li):
    bn, l_new, _ = proj.shape
    hd = A_HEAD_DIM

    def col(cb):
        return pl.BlockSpec((None, l_new, hd), lambda b, h, cb=cb: (b, 0, cb * A_HEADS + h))

    def cache_spec(n_buf, kv):
        return pl.BlockSpec((None, None, n_buf, hd),
                            lambda b, h, kv=kv: (li, b, 0, kv * A_HEADS + h))

    in_specs = [col(3 * g + j) for g in range(3) for j in range(3)]
    args = [proj] * 9
    for g in range(3):
        n_buf = caches[g].shape[2]
        in_specs += [cache_spec(n_buf, 0), cache_spec(n_buf, 1)]
        args += [caches[g], caches[g]]
    return pl.pallas_call(
        _attn_sample_kernel,
        grid=(bn, A_HEADS),
        in_specs=in_specs,
        out_specs=pl.BlockSpec((None, l_new, hd), lambda b, h: (b, 0, h)),
        out_shape=jax.ShapeDtypeStruct((bn, l_new, A_WIDTH), F32),
        compiler_params=_cparams(("parallel", "parallel"), 32),
        name="attn_sample",
    )(*args)


def _gla_kernel(bq_ref, bf_ref, bi_ref, bg_ref, lbp_ref, gain_ref, s0_ref,
                o_ref, sfin_ref, st_scr, att_scr, qkb_scr, *, li, cin):
    c = pl.program_id(2)
    ch = B_CHUNK
    last = c == pl.num_programs(2) - 1

    def pad(x):
        if cin == ch:
            return x
        return jnp.concatenate([x, jnp.zeros((ch - cin, x.shape[1]), x.dtype)], axis=0)

    row = lax.broadcasted_iota(jnp.int32, (ch, B_DIM), 0)
    lane = lax.broadcasted_iota(jnp.int32, (B_SUB, ch), 1)
    row_sub = lax.broadcasted_iota(jnp.int32, (B_SUB, B_DIM), 0)
    live = row < cin
    tri = (row >= lax.broadcasted_iota(jnp.int32, (ch, ch), 1)).astype(F32)

    @pl.when(c == 0)
    def _():
        for hj in range(B_HEADS_PER_STEP):
            st_scr[hj] = s0_ref[hj].T

    for hj in range(B_HEADS_PER_STEP):
        hs = slice(hj * B_DIM, (hj + 1) * B_DIM)
        lbp = lbp_ref[:, hs]
        e = jnp.exp(lbp - jnp.max(lbp, axis=0, keepdims=True))
        soft = e / jnp.sum(e, axis=0, keepdims=True)
        lb = jnp.sum(soft[0:li + 1], axis=0, keepdims=True) - soft[0:1]

        q = _silu(pad(bq_ref[:, hs]))
        fgate = lb + (1.0 - lb) * _sigmoid(pad(bf_ref[:, hs]))
        g = jnp.where(live, jnp.log(fgate), 0.0)
        kk = jnp.where(live, 1.0 - fgate, 0.0)
        v16 = pad(bi_ref[:, hs]).astype(BF16)

        b = jnp.dot(tri, g, preferred_element_type=F32, precision=lax.Precision.HIGHEST)
        b_last = b[ch - 1:ch]
        st = st_scr[hj]

        o = _dot_nt((q * jnp.exp(b)).astype(BF16), st.astype(BF16))

        q_scr, kk_scr, b_scr = (qkb_scr.at[hj, j] for j in range(3))
        q_scr[...] = q
        kk_scr[...] = kk
        b_scr[...] = b

        for qb in range(ch // B_SUB):
            r0 = qb * B_SUB
            q_s = q_scr[r0:r0 + B_SUB, :]
            b_s = b_scr[r0:r0 + B_SUB, :]
            kk_s = kk_scr[r0:r0 + B_SUB, :]
            if qb > 0:
                b_ref = b_scr[r0 - 1:r0, :]
                qf = q_s * jnp.exp(b_s - b_ref)
                earlier = row < r0
                kf = jnp.where(earlier, kk_scr[...] * jnp.exp(jnp.where(earlier, b_ref - b_scr[...], 0.0)), 0.0)
                att = _dot_nt(qf.astype(BF16), kf.astype(BF16))
            else:
                att = jnp.zeros((B_SUB, ch), F32)
            for s in range(B_SUB):
                causal = row_sub >= s
                dec = jnp.where(causal, jnp.exp(jnp.where(causal, b_s - b_s[s:s + 1], 0.0)), 0.0)
                colv = jnp.sum(q_s * dec * kk_s[s:s + 1], axis=-1, keepdims=True)
                att = jnp.where(lane == r0 + s, colv, att)
            att_scr[hj, r0:r0 + B_SUB, :] = att

        o = o + _dot(att_scr[hj].astype(BF16), v16)

        kd = kk * jnp.exp(b_last - b)
        st_new = st * jnp.exp(b_last) + _dot_tn(v16, kd.astype(BF16))
        st_scr[hj] = st_new

        on = _rms(o, gain_ref[:, hs]) * _silu(pad(bg_ref[:, hs]))
        o_ref[:, hs] = on[0:cin].astype(o_ref.dtype)

    @pl.when(last)
    def _():
        for hj in range(B_HEADS_PER_STEP):
            sfin_ref[hj] = st_scr[hj].T


def gla(proj, lb_params, b_gain, s0, li):
    bn, t_len, ab_in = proj.shape
    n_ab = lb_params.shape[0]
    cin = min(B_CHUNK, t_len)
    nc = t_len // cin
    base = ab_in // B_DIM - 4 * B_HEADS

    hp = B_HEADS_PER_STEP
    wid = hp * B_DIM
    base = base // hp

    def col(j):
        return pl.BlockSpec((None, cin, wid), lambda b, h, c, j=j: (b, c, base + j * (B_HEADS // hp) + h))

    out_dtype = F32 if cin < 16 else BF16
    return pl.pallas_call(
        functools.partial(_gla_kernel, li=li, cin=cin),
        grid=(bn, B_HEADS // hp, nc),
        in_specs=[col(0), col(1), col(2), col(3),
                  pl.BlockSpec((n_ab, wid), lambda b, h, c: (0, h)),
                  pl.BlockSpec((None, 1, wid), lambda b, h, c: (li, 0, h)),
                  pl.BlockSpec((None, hp, B_DIM, B_DIM), lambda b, h, c: (b, h, 0, 0))],
        out_specs=[pl.BlockSpec((None, cin, wid), lambda b, h, c: (b, c, h)),
                   pl.BlockSpec((None, hp, B_DIM, B_DIM), lambda b, h, c: (b, h, 0, 0))],
        out_shape=[jax.ShapeDtypeStruct((bn, t_len, B_HEADS * B_DIM), out_dtype),
                   jax.ShapeDtypeStruct((bn, B_HEADS, B_DIM, B_DIM), F32)],
        scratch_shapes=[pltpu.VMEM((hp, B_DIM, B_DIM), F32), pltpu.VMEM((hp, B_CHUNK, B_CHUNK), F32),
                        pltpu.VMEM((hp, 3, B_CHUNK, B_DIM), F32)],
        compiler_params=_cparams(("parallel", "parallel", "arbitrary"), 32),
        name="gla",
    )(proj, proj, proj, proj, lb_params, b_gain.reshape(n_ab, 1, -1), s0)


def _mix_kernel(x_ref, xp_ref, sh_ref, g_ref, mu_ref, *outs):
    xm_refs, hl_ref = outs[:6], outs[6]
    i = pl.program_id(1)
    tt = x_ref.shape[0]
    g = g_ref[...]
    h = _rms(x_ref[...], g)
    hp = _rms(xp_ref[...], g)
    first = jnp.where(i == 0, sh_ref[...], hp[7:8])
    rolled = pltpu.roll(h, 1, axis=0)
    row = lax.broadcasted_iota(jnp.int32, h.shape, 0)
    prev = jnp.where(row == 0, first, rolled)
    xx = prev - h
    for j in range(6):
        xm_refs[j][...] = (h + xx * mu_ref[j:j + 1]).astype(BF16)

    @pl.when(i == pl.num_programs(1) - 1)
    def _():
        hl_ref[...] = h[tt - 1:tt]


def rwkv_mix(x, shift, g, mu):
    bn, t_len, d = x.shape
    tt = min(t_len, 256)
    sub = tt // 8
    tile = pl.BlockSpec((None, tt, d), lambda b, i: (b, i, 0))
    *xm, hl = pl.pallas_call(
        _mix_kernel,
        grid=(bn, t_len // tt),
        in_specs=[pl.BlockSpec((None, tt, d), lambda b, i: (b, i, 0)),
                  pl.BlockSpec((None, 8, d), lambda b, i: (b, jnp.maximum(i * sub - 1, 0), 0)),
                  pl.BlockSpec((None, 1, d), lambda b, i: (b, 0, 0)),
                  pl.BlockSpec((1, d), lambda b, i: (0, 0)),
                  pl.BlockSpec((6, d), lambda b, i: (0, 0))],
        out_specs=[tile] * 6 + [pl.BlockSpec((None, 1, d), lambda b, i: (b, 0, 0))],
        out_shape=[jax.ShapeDtypeStruct((bn, t_len, d), BF16)] * 6 + [jax.ShapeDtypeStruct((bn, 1, d), F32)],
        compiler_params=_cparams(("parallel", "arbitrary"), 40),
        name="rwkv_mix",
    )(x, x, shift.reshape(bn, 1, d), g.reshape(1, d), mu)
    return [t.reshape(bn * t_len, d) for t in xm], hl.reshape(bn, d)


def _scan_kernel(r_ref, k_ref, v_ref, wp_ref, ap_ref, par_ref, s0_ref,
                 o_ref, sfin_ref, st, ops, *, tc):
    c = pl.program_id(1)
    n = C_HEAD_DIM

    @pl.when(c == 0)
    def _():
        st[...] = s0_ref[...]

    def key_row(j, ki):
        return ops[j, pl.ds(ki, 1), :]

    def step(t, carry):
        c_kk, c_ka, c_rk, ln_w, ln_b = (par_ref[j] for j in range(5))
        z = -wp_ref[t]
        softplus = jnp.maximum(z, 0.0) + jnp.log(1.0 + jnp.exp(-jnp.abs(z)))
        w = jnp.exp(-jnp.exp(-softplus - 0.5))
        a = _sigmoid(ap_ref[t])
        kt = k_ref[t]
        kk = kt * c_kk
        nrm = jnp.sqrt(jnp.sum(kk * kk, axis=0, keepdims=True))
        kk = kk / jnp.maximum(nrm, 1e-12)
        k2 = kt * (1.0 + (a - 1.0) * c_ka)
        rt = r_ref[t]
        vt = v_ref[t]
        ops[0] = -kk
        ops[1] = w
        ops[2] = kk * a
        ops[3] = k2
        ops[4] = rt
        sa0 = st[0] * key_row(0, 0)
        sa1 = st[1] * key_row(0, 1)
        for ki in range(2, n, 2):
            sa0 = sa0 + st[ki] * key_row(0, ki)
            sa1 = sa1 + st[ki + 1] * key_row(0, ki + 1)
        sa = sa0 + sa1
        o0 = o1 = None
        for ki in range(n):
            s_n = st[ki] * key_row(1, ki) + sa * key_row(2, ki) + vt * key_row(3, ki)
            st[ki] = s_n
            term = s_n * key_row(4, ki)
            if ki % 2 == 0:
                o0 = term if o0 is None else o0 + term
            else:
                o1 = term if o1 is None else o1 + term
        o = o0 + o1
        mean = jnp.mean(o, axis=0, keepdims=True)
        var = jnp.mean(jnp.square(o - mean), axis=0, keepdims=True)
        o = (o - mean) * lax.rsqrt(var + C_GN_EPS) * ln_w + ln_b
        bonus = jnp.sum(rt * k2 * c_rk, axis=0, keepdims=True)
        o_ref[t] = o + bonus * vt
        return carry

    lax.fori_loop(0, tc, step, 0)

    @pl.when(c == pl.num_programs(1) - 1)
    def _():
        sfin_ref[...] = st[...]


def rwkv_scan(r, k, v, wp, ap, par, s0):
    t_len, n, bh = r.shape
    tc = min(t_len, 32)
    seq = pl.BlockSpec((tc, n, LANES), lambda l, c: (c, 0, l))
    return pl.pallas_call(
        functools.partial(_scan_kernel, tc=tc),
        grid=(bh // LANES, t_len // tc),
        in_specs=[seq] * 5 + [pl.BlockSpec((5, n, LANES), lambda l, c: (0, 0, l)),
                              pl.BlockSpec((n, n, LANES), lambda l, c: (0, 0, l))],
        out_specs=[seq, pl.BlockSpec((n, n, LANES), lambda l, c: (0, 0, l))],
        out_shape=[jax.ShapeDtypeStruct((t_len, n, bh), F32),
                   jax.ShapeDtypeStruct((n, n, bh), F32)],
        scratch_shapes=[pltpu.VMEM((n, n, LANES), F32), pltpu.VMEM((5, n, LANES), F32)],
        compiler_params=_cparams(("parallel", "arbitrary"), 40),
        name="rwkv_scan",
    )(r, k, v, wp, ap, par, s0)


def _tiles_time(t_len):
    return t_len % 256 == 0


def _pad_to(x, axis, mult):
    pad = (-x.shape[axis]) % mult
    if pad == 0:
        return x
    widths = [(0, 0)] * x.ndim
    widths[axis] = (0, pad)
    return jnp.pad(x, widths)


def _ab_layer(x, h, li, p, caches, b_state):
    bn, t_len, d = x.shape
    proj = matmul(h, p["w_in_ab"][li]).reshape(bn, t_len, -1)
    kv_new = []
    for g in range(3):
        kv = proj[:, :, (3 * g + 1) * A_WIDTH:(3 * g + 3) * A_WIDTH]
        kv_new.append(kv.reshape(bn, t_len, 2, A_HEADS, A_HEAD_DIM))
    if caches is None:
        a_out = attn_prompt(proj)
        new_bufs = [kv_new[g][:, -min(A_GROUPS[g][0], t_len):] for g in range(3)]
        s0 = jnp.zeros((bn, B_HEADS, B_DIM, B_DIM), F32)
    else:
        flat = [c.reshape(c.shape[0], c.shape[1], c.shape[2], -1) for c in caches]
        a_out = attn_sample(proj, flat, li).astype(BF16)
        new_bufs = [jnp.concatenate([caches[g][li], kv_new[g]], axis=1)[:, t_len:] for g in range(3)]
        s0 = b_state[li]
    b_out, s_b = gla(proj, p["b_lower_bounds"], p["b_norm_gain"], s0, li)
    mix_in = jnp.concatenate([a_out, b_out.astype(BF16)], axis=-1).reshape(bn * t_len, -1)
    return mix_in, new_bufs, s_b


def _rwkv_layer(x, li, g_pre, p, shift, wkv):
    bn, t_len, d = x.shape
    n = C_HEAD_DIM
    hh = d // n
    (xr, xw, xk, xv, xa, xg), h_last = rwkv_mix(x, shift, g_pre, p["c_mu"][li])

    def proj_tm(a, w, bias=None):
        if _tiles_time(t_len):
            return matmul(a, w, bias=bias, tb=(bn, t_len))
        y = matmul(a, w, bias=bias)
        return jnp.swapaxes(y.reshape(bn, t_len, -1), 0, 1).reshape(t_len, -1)

    w_rkv = p["c_w_rkv"][li]
    r = proj_tm(xr, w_rkv[0])
    k = proj_tm(xk, w_rkv[1])
    v = proj_tm(xv, w_rkv[2])
    w_lo = matmul(xw, _pad_to(p["c_w1"][li], 1, LANES), act="tanh", out_dtype=BF16)
    wp = proj_tm(w_lo, _pad_to(p["c_w2"][li], 0, LANES), bias=p["c_w0"][li])
    a_lo = matmul(xa, _pad_to(p["c_a1"][li], 1, LANES), out_dtype=BF16)
    ap = proj_tm(a_lo, _pad_to(p["c_a2"][li], 0, LANES), bias=p["c_a0"][li])
    g_lo = matmul(xg, _pad_to(p["c_g1"][li], 1, LANES), act="sigmoid", out_dtype=BF16)
    gate = proj_tm(g_lo, _pad_to(p["c_g2"][li], 0, LANES))

    def to_scan(t):
        return jnp.swapaxes(t.reshape(t_len, bn * hh, n), 1, 2)

    def lanes(vec):
        return jnp.tile(vec.reshape(hh, n).T, (1, bn))

    par = jnp.stack([lanes(p["c_k_k"][li]), lanes(p["c_k_a"][li]), lanes(p["c_r_k"][li].reshape(-1)),
                     lanes(p["c_ln_w"][li]), lanes(p["c_ln_b"][li])])
    s0 = jnp.transpose(wkv, (3, 2, 0, 1)).reshape(n, n, bn * hh)
    o, s_fin = rwkv_scan(*(to_scan(t) for t in (r, k, v, wp, ap)), par, s0)
    o = jnp.swapaxes(o, 1, 2).reshape(t_len, bn * d)
    s_fin = jnp.transpose(s_fin.reshape(n, n, bn, hh), (2, 3, 1, 0))
    return o, gate, s_fin, h_last


def _trunk(x, caches, b_states, c_wkv, c_shift, p):
    bn, t_len, d = x.shape
    m = bn * t_len
    depth = p["norm_gains"].shape[0]
    new_a = ([], [], [])
    new_b, new_wkv, new_shift = [], [], []
    h = rmsnorm_rows(x.reshape(m, d), p["norm_gains"][0, 0])
    for layer in range(depth):
        gains = p["norm_gains"][layer]
        li = layer // 2
        last = layer == depth - 1
        next_pre = p["norm_gains"][layer + 1, 0] if not last else gains[0]
        if layer % 2 == 0:
            mix_in, kv_bufs, s_b = _ab_layer(x, h, li, p, caches, b_states)
            for lst, kv in zip(new_a, kv_bufs):
                lst.append(kv)
            new_b.append(s_b)
            x2, h2 = matmul_post(mix_in, p["w_out_ab"][li], gains[1], x.reshape(m, d), gains[2])
        else:
            wkv0 = jnp.zeros((bn, d // C_HEAD_DIM, C_HEAD_DIM, C_HEAD_DIM), F32) if c_wkv is None else c_wkv[li]
            sh0 = jnp.zeros((bn, d), F32) if c_shift is None else c_shift[li]
            o, gate, s_wkv, s_shift = _rwkv_layer(x, li, gains[0], p, sh0, wkv0)
            new_wkv.append(s_wkv)
            new_shift.append(s_shift)
            if _tiles_time(t_len):
                tb = (bn, t_len)
            else:
                tb = None
                o, gate = (jnp.swapaxes(t.reshape(t_len, bn, d), 0, 1).reshape(m, d) for t in (o, gate))
            x2, h2 = matmul_post(o, p["c_w_out"][li], gains[1], x.reshape(m, d), gains[2], gate=gate, tb=tb)
        next_even = (layer + 1) % 2 == 0 and not last
        x3, h = ffn(h2, p["w_ffn_up"][layer], p["w_ffn_down"][layer], gains[3], x2, next_pre,
                    emit_h=next_even)
        x = x3.reshape(bn, t_len, d)
    return (x, tuple(jnp.stack(lst) for lst in new_a), jnp.stack(new_b),
            jnp.stack(new_wkv), jnp.stack(new_shift))


def kernel(x_prompt, x_sample, cache_a1_kv, cache_a2_kv, cache_a3_kv, state_b, state_c_wkv, state_c_shift, norm_gains, w_in_ab, w_out_ab, b_lower_bounds, b_norm_gain, c_mu, c_w_rkv, c_w0, c_w1, c_w2, c_a0, c_a1, c_a2, c_g1, c_g2, c_k_k, c_k_a, c_r_k, c_ln_w, c_ln_b, c_w_out, w_ffn_up, w_ffn_down):
    to16 = lambda w: w.astype(BF16)
    p = dict(norm_gains=norm_gains, w_in_ab=to16(w_in_ab), w_out_ab=to16(w_out_ab),
             b_lower_bounds=b_lower_bounds, b_norm_gain=b_norm_gain, c_mu=c_mu,
             c_w_rkv=to16(c_w_rkv), c_w0=c_w0, c_w1=to16(c_w1), c_w2=to16(c_w2),
             c_a0=c_a0, c_a1=to16(c_a1), c_a2=to16(c_a2), c_g1=to16(c_g1), c_g2=to16(c_g2),
             c_k_k=c_k_k, c_k_a=c_k_a, c_r_k=c_r_k, c_ln_w=c_ln_w, c_ln_b=c_ln_b,
             c_w_out=to16(c_w_out), w_ffn_up=to16(w_ffn_up), w_ffn_down=to16(w_ffn_down))
    y_p, (pa1, pa2, pa3), pb, pwkv, pshift = _trunk(x_prompt, None, None, None, None, p)
    y_s, (sa1, sa2, sa3), sb, swkv, sshift = _trunk(
        x_sample, (cache_a1_kv, cache_a2_kv, cache_a3_kv), state_b, state_c_wkv, state_c_shift, p)
    return (y_p, y_s, pa1, pa2, pa3, pb, pwkv, pshift, sa1, sa2, sa3, sb, swkv, sshift)
```

```python
import functools

import jax
import jax.numpy as jnp
from jax import lax
from jax.experimental import pallas as pl
from jax.experimental.pallas import tpu as pltpu

F32 = jnp.float32
BF16 = jnp.bfloat16

NORM_EPS = 1e-6
A_GROUPS = ((128, 1), (512, 4), (2048, 16))
A_HEADS = 8
A_HEAD_DIM = 128
A_WIDTH = A_HEADS * A_HEAD_DIM
A_BLK = 128
B_HEADS = 8
B_DIM = 128
B_CHUNK = 128
B_SUB = 8
B_HEADS_PER_STEP = 2
C_HEAD_DIM = 64
C_GN_EPS = 64e-5
LANES = 128
MIB = 1024 * 1024


def _cparams(sem, vmem_mib):
    return pltpu.CompilerParams(dimension_semantics=sem, vmem_limit_bytes=vmem_mib * MIB)


def _rms(x, g):
    return x * lax.rsqrt(jnp.mean(x * x, axis=-1, keepdims=True) + NORM_EPS) * g


def _dot(a, b):
    return jnp.dot(a, b, preferred_element_type=F32)


def _dot_nt(a, b):
    return lax.dot_general(a, b, (((1,), (1,)), ((), ())), preferred_element_type=F32)


def _dot_tn(a, b):
    return lax.dot_general(a, b, (((0,), (0,)), ((), ())), preferred_element_type=F32)


def _log2(n):
    assert n > 0 and n & (n - 1) == 0, n
    return n.bit_length() - 1


def _sigmoid(x):
    return 1.0 / (1.0 + jnp.exp(-x))


def _silu(x):
    return x * _sigmoid(x)


def _norm_kernel(x_ref, g_ref, o_ref):
    o_ref[...] = _rms(x_ref[...], g_ref[...]).astype(o_ref.dtype)


def rmsnorm_rows(x, g):
    m, d = x.shape
    tm = min(m, 512)
    return pl.pallas_call(
        _norm_kernel,
        grid=(m // tm,),
        in_specs=[pl.BlockSpec((tm, d), lambda i: (i, 0)),
                  pl.BlockSpec((1, d), lambda i: (0, 0))],
        out_specs=pl.BlockSpec((tm, d), lambda i: (i, 0)),
        out_shape=jax.ShapeDtypeStruct((m, d), BF16),
        compiler_params=_cparams(("parallel",), 32),
        name="rmsnorm_rows",
    )(x, g.reshape(1, d))


def _mm_kernel(*refs, act, has_bias, transpose_out):
    if has_bias:
        a_ref, w_ref, b_ref, o_ref = refs
    else:
        a_ref, w_ref, o_ref = refs
    if transpose_out:
        y = _dot_nt(w_ref[...], a_ref[...])
    else:
        y = _dot(a_ref[...], w_ref[...])
    if has_bias:
        y = y + b_ref[...]
    if act == "tanh":
        y = jnp.tanh(y)
    elif act == "sigmoid":
        y = _sigmoid(y)
    o_ref[...] = y.astype(o_ref.dtype)


def matmul(a, w, bias=None, act=None, out_dtype=F32, tb=None):
    m, k = a.shape
    n = w.shape[1]
    tn = min(n, 1024)
    nj = n // tn
    if tb is None:
        tm = min(m, 1024)
        out_shape = (m, n)
        out_block = (tm, tn)
        out_index = lambda i, j: (i, j)
    else:
        bn, t_len = tb
        tm = min(t_len, 1024)
        nt = t_len // tm
        out_shape = (bn * n, t_len)
        out_block = (tn, tm)
        out_index = lambda i, j: ((i // nt) * nj + j, i % nt)
    in_specs = [pl.BlockSpec((tm, k), lambda i, j: (i, 0))]
    if tb is None:
        in_specs.append(pl.BlockSpec((k, tn), lambda i, j: (0, j)))
        args = [a, w]
    else:
        in_specs.append(pl.BlockSpec((tn, k), lambda i, j: (j, 0)))
        args = [a, w.T]
    if bias is not None:
        if tb is None:
            in_specs.append(pl.BlockSpec((1, tn), lambda i, j: (0, j)))
            args.append(bias.reshape(1, n))
        else:
            in_specs.append(pl.BlockSpec((tn, 1), lambda i, j: (j, 0)))
            args.append(bias.reshape(n, 1))
    return pl.pallas_call(
        functools.partial(_mm_kernel, act=act, has_bias=bias is not None, transpose_out=tb is not None),
        grid=(m // tm, nj),
        in_specs=in_specs,
        out_specs=pl.BlockSpec(out_block, out_index),
        out_shape=jax.ShapeDtypeStruct(out_shape, out_dtype),
        compiler_params=_cparams(("parallel", "parallel"), 40),
        name="matmul",
    )(*args)


def _mm_post_kernel(*refs, has_gate, emit_h, transpose_in):
    refs = list(refs)
    a_ref = refs.pop(0)
    gate_ref = refs.pop(0) if has_gate else None
    w_ref, g1_ref, xres_ref, gn_ref, x_out = refs[:5]
    a = a_ref[...]
    if has_gate:
        a = a * gate_ref[...]
    a = a.astype(BF16)
    y = _dot_tn(a, w_ref[...]) if transpose_in else _dot(a, w_ref[...])
    x_new = xres_ref[...] + _rms(y, g1_ref[...])
    x_out[...] = x_new
    if emit_h:
        refs[5][...] = _rms(x_new, gn_ref[...]).astype(BF16)


def matmul_post(a, w, g_post, xres, g_next, gate=None, emit_h=True, tb=None):
    m, d = xres.shape
    k = w.shape[0]
    tm = min(m, 256) if tb is None else min(tb[1], 256)
    row = lambda i: (i, 0)
    fixed = lambda i: (0, 0)
    if tb is None:
        a_spec = pl.BlockSpec((tm, k), row)
    else:
        nt = tb[1] // tm
        a_spec = pl.BlockSpec((k, tm), lambda i: (i // nt, i % nt))
    in_specs = [a_spec]
    args = [a]
    if gate is not None:
        in_specs.append(a_spec)
        args.append(gate)
    in_specs += [pl.BlockSpec((k, d), fixed), pl.BlockSpec((1, d), fixed),
                 pl.BlockSpec((tm, d), row), pl.BlockSpec((1, d), fixed)]
    args += [w, g_post.reshape(1, d), xres, g_next.reshape(1, d)]
    out_specs = [pl.BlockSpec((tm, d), row)]
    out_shape = [jax.ShapeDtypeStruct((m, d), F32)]
    if emit_h:
        out_specs.append(pl.BlockSpec((tm, d), row))
        out_shape.append(jax.ShapeDtypeStruct((m, d), BF16))
    res = pl.pallas_call(
        functools.partial(_mm_post_kernel, has_gate=gate is not None, emit_h=emit_h, transpose_in=tb is not None),
        grid=(m // tm,),
        in_specs=in_specs,
        out_specs=out_specs,
        out_shape=out_shape,
        compiler_params=_cparams(("parallel",), 48),
        name="matmul_post",
    )(*args)
    return (res[0], res[1]) if emit_h else (res[0], None)


def _ffn_kernel(h_ref, wg_ref, wu_ref, wd_ref, g1_ref, xres_ref, gn_ref, *outs, emit_h):
    if emit_h:
        x_out, h_out, acc_ref = outs
    else:
        x_out, acc_ref = outs
    j = pl.program_id(1)

    @pl.when(j == 0)
    def _():
        acc_ref[...] = jnp.zeros_like(acc_ref)

    h = h_ref[...]
    gate = _dot(h, wg_ref[...])
    up = _dot(h, wu_ref[...])
    act = (_silu(gate) * up).astype(BF16)
    acc_ref[...] += _dot(act, wd_ref[...])

    @pl.when(j == pl.num_programs(1) - 1)
    def _():
        x_new = xres_ref[...] + _rms(acc_ref[...], g1_ref[...])
        x_out[...] = x_new
        if emit_h:
            h_out[...] = _rms(x_new, gn_ref[...]).astype(BF16)


def ffn(h, w_up, w_down, g_post, xres, g_next, emit_h=True):
    m, d = h.shape
    f = w_down.shape[0]
    tm = min(m, 512)
    th = 512
    nj = f // th
    row = lambda i, j: (i, 0)
    fixed = lambda i, j: (0, 0)
    in_specs = [pl.BlockSpec((tm, d), row),
                pl.BlockSpec((d, th), lambda i, j: (0, j)),
                pl.BlockSpec((d, th), lambda i, j: (0, j + nj)),
                pl.BlockSpec((th, d), lambda i, j: (j, 0)),
                pl.BlockSpec((1, d), fixed),
                pl.BlockSpec((tm, d), row),
                pl.BlockSpec((1, d), fixed)]
    out_specs = [pl.BlockSpec((tm, d), row)]
    out_shape = [jax.ShapeDtypeStruct((m, d), F32)]
    if emit_h:
        out_specs.append(pl.BlockSpec((tm, d), row))
        out_shape.append(jax.ShapeDtypeStruct((m, d), BF16))
    res = pl.pallas_call(
        functools.partial(_ffn_kernel, emit_h=emit_h),
        grid=(m // tm, nj),
        in_specs=in_specs,
        out_specs=out_specs,
        out_shape=out_shape,
        scratch_shapes=[pltpu.VMEM((tm, d), F32)],
        compiler_params=_cparams(("parallel", "arbitrary"), 52),
        name="ffn",
    )(h, w_up, w_up, w_down, g_post.reshape(1, d), xres, g_next.reshape(1, d))
    return (res[0], res[1]) if emit_h else (res[0], None)


def _alibi_slope(shape):
    h = pl.program_id(1)
    expo = (h + 1).astype(F32) * (-8.0 / A_HEADS)
    return jnp.exp2(jnp.full(shape, 1.0, F32) * expo)


def _attn_prompt_kernel(q1, k1, v1, q2, k2, v2, q3, k3, v3, out_ref, o_scr, l_scr):
    t_len = out_ref.shape[0]
    blk = A_BLK
    scale = A_HEAD_DIM ** -0.5
    slope = _alibi_slope((blk, blk))
    qi = lax.broadcasted_iota(jnp.int32, (blk, blk), 0)
    ki = lax.broadcasted_iota(jnp.int32, (blk, blk), 1)
    dist_cur = (qi - ki).astype(F32)
    dist_prev = (qi - ki + blk).astype(F32)
    mask_cur = ki <= qi
    mask_prev = ki >= qi
    neg_inf = jnp.float32(-jnp.inf)

    def rows(start, dil):
        if dil == 1:
            return pl.ds(pl.multiple_of(start, blk), blk)
        return pl.ds(start, blk, stride=dil)

    def one_block(gi, q_ref, k_ref, v_ref, dil, start, prev_start, use_prev):
        q = q_ref[rows(start, dil), :].astype(BF16)
        kc = k_ref[rows(start, dil), :].astype(BF16)
        vc = v_ref[rows(start, dil), :].astype(BF16)
        s_c = _dot_nt(q, kc) * scale - slope * (dist_cur * float(dil))
        s_c = jnp.where(mask_cur, s_c, neg_inf)
        mx = jnp.max(s_c, axis=-1, keepdims=True)
        if use_prev is not None:
            kp = k_ref[rows(prev_start, dil), :].astype(BF16)
            vp = v_ref[rows(prev_start, dil), :].astype(BF16)
            s_p = _dot_nt(q, kp) * scale - slope * (dist_prev * float(dil))
            s_p = jnp.where(mask_prev, s_p + jnp.where(use_prev, 0.0, neg_inf), neg_inf)
            mx = jnp.maximum(mx, jnp.max(s_p, axis=-1, keepdims=True))
        p_c = jnp.exp(s_c - mx)
        den = jnp.sum(p_c, axis=-1, keepdims=True)
        acc = _dot(p_c.astype(BF16), vc)
        if use_prev is not None:
            p_p = jnp.exp(s_p - mx)
            den = den + jnp.sum(p_p, axis=-1, keepdims=True)
            acc = acc + _dot(p_p.astype(BF16), vp)
        o = acc / den
        lse = mx + jnp.log(den)
        o_scr[gi, rows(start, dil), :] = o
        l_scr[gi, rows(start, dil), :] = jnp.broadcast_to(lse, (blk, LANES))

    for gi, (q_ref, k_ref, v_ref) in enumerate(((q1, k1, v1), (q2, k2, v2), (q3, k3, v3))):
        window, dil = A_GROUPS[gi]
        assert window // dil == blk
        m_len = t_len // dil
        nb = m_len // blk
        assert nb * blk * dil == t_len

        def body(idx, carry, gi=gi, q_ref=q_ref, k_ref=k_ref, v_ref=v_ref, dil=dil, nb=nb):
            r = idx // nb
            n = idx - r * nb
            start = n * (blk * dil) + r
            if nb == 1:
                one_block(gi, q_ref, k_ref, v_ref, dil, start, None, None)
            else:
                prev_start = jnp.maximum(n - 1, 0) * (blk * dil) + r
                one_block(gi, q_ref, k_ref, v_ref, dil, start, prev_start, n > 0)
            return carry

        lax.fori_loop(0, dil * nb, body, 0, unroll=8)

    def merge(c, carry):
        sl = pl.ds(pl.multiple_of(c * blk, blk), blk)
        l0, l1, l2 = l_scr[0, sl, :], l_scr[1, sl, :], l_scr[2, sl, :]
        mx = jnp.maximum(jnp.maximum(l0, l1), l2)
        w0, w1, w2 = jnp.exp(l0 - mx), jnp.exp(l1 - mx), jnp.exp(l2 - mx)
        num = w0 * o_scr[0, sl, :] + w1 * o_scr[1, sl, :] + w2 * o_scr[2, sl, :]
        out_ref[sl, :] = (num / (w0 + w1 + w2)).astype(out_ref.dtype)
        return carry

    lax.fori_loop(0, t_len // blk, merge, 0, unroll=2)


def attn_prompt(proj):
    bn, t_len, _ = proj.shape
    hd = A_HEAD_DIM

    def col(cb):
        return pl.BlockSpec((None, t_len, hd), lambda b, h, cb=cb: (b, 0, cb * A_HEADS + h))

    in_specs = [col(3 * g + j) for g in range(3) for j in range(3)]
    return pl.pallas_call(
        _attn_prompt_kernel,
        grid=(bn, A_HEADS),
        in_specs=in_specs,
        out_specs=pl.BlockSpec((None, t_len, hd), lambda b, h: (b, 0, h)),
        out_shape=jax.ShapeDtypeStruct((bn, t_len, A_WIDTH), BF16),
        scratch_shapes=[pltpu.VMEM((3, t_len, hd), F32), pltpu.VMEM((3, t_len, LANES), F32)],
        compiler_params=_cparams(("parallel", "parallel"), 40),
        name="attn_prompt",
    )(*([proj] * 9))


KV_ROWS = 2 * A_HEADS
SAMPLE_CHUNKS = 4


def _attn_sample_kernel(*refs, n_prev):
    cols = refs[0:9]
    chunk_refs = refs[9:12]
    next_refs = refs[12:15]
    out_ref = refs[15 + n_prev]
    new_refs = refs[16 + n_prev:19 + n_prev]
    m_scr, d_scr, acc_scr, new_scr = refs[19 + n_prev:]
    c = pl.program_id(1)
    n_chunks = pl.num_programs(1)
    last = c == n_chunks - 1
    l_new = out_ref.shape[0]
    shift = l_new * KV_ROWS
    hd = A_HEAD_DIM
    scale = hd ** -0.5
    neg_inf = jnp.float32(-jnp.inf)

    nq = A_HEADS * l_new

    @pl.when(c == 0)
    def _():
        m_scr[...] = jnp.full(m_scr.shape, neg_inf, F32)
        d_scr[...] = jnp.zeros(d_scr.shape, F32)
        acc_scr[...] = jnp.zeros(acc_scr.shape, F32)
        for gi in range(3):
            for kv, src in enumerate(cols[3 * gi + 1:3 * gi + 3]):
                for h in range(A_HEADS):
                    new_scr[gi, pl.ds(kv * A_HEADS + h, l_new, stride=KV_ROWS), :] = src[:, h * hd:(h + 1) * hd]

    def segment(q_all, buf, pos0, n_buf, window, dil, live):
        rows = buf.shape[0]
        keys_on_value_rows = jnp.concatenate([jnp.zeros((A_HEADS, hd), F32), buf[0:rows - A_HEADS]], axis=0)
        s = _dot_nt(q_all, keys_on_value_rows.astype(BF16))
        qrow = lax.broadcasted_iota(jnp.int32, (nq, rows), 0)
        col = lax.broadcasted_iota(jnp.int32, (nq, rows), 1)
        head = qrow >> _log2(l_new)
        dist = n_buf + (qrow & (l_new - 1)) - (pos0 + (col >> _log2(KV_ROWS)))
        valid = jnp.logical_and(
            jnp.logical_and((col & (KV_ROWS - 1)) == A_HEADS + head, (dist & (dil - 1)) == 0),
            jnp.logical_and(dist >= 0, dist <= window))
        slope = jnp.exp2((head + 1).astype(F32) * (-8.0 / A_HEADS))
        s = jnp.where(valid, s * scale - slope * dist.astype(F32), neg_inf)
        if live is not None:
            s = jnp.where(live, s, neg_inf)
        m_old = m_scr[...]
        m_new = jnp.maximum(m_old, jnp.max(s, axis=-1, keepdims=True))
        alpha = jnp.exp(m_old - m_new)
        p = jnp.exp(s - m_new[:, 0:1])
        d_scr[...] = alpha * d_scr[...] + jnp.sum(p, axis=-1, keepdims=True)
        acc_scr[...] = alpha * acc_scr[...] + _dot(p.astype(BF16), buf.astype(BF16))
        m_scr[...] = m_new

    for gi in range(3):
        window, dil = A_GROUPS[gi]
        q_ref = cols[3 * gi]
        c_ref, n_ref, o_ref = chunk_refs[gi], next_refs[gi], new_refs[gi]
        rows = c_ref.shape[0]
        p_len = rows // KV_ROWS
        n_buf = p_len * SAMPLE_CHUNKS

        o_ref[0:rows - shift, :] = c_ref[shift:rows, :]
        o_ref[rows - shift:rows, :] = jnp.where(last, new_scr[gi], n_ref[...])

        q_all = jnp.concatenate([q_ref[:, h * hd:(h + 1) * hd] for h in range(A_HEADS)], axis=0).astype(BF16)
        segment(q_all, new_scr[gi], n_buf, n_buf, window, dil, c == 0)
        segment(q_all, c_ref[...], c * p_len, n_buf, window, dil, None)

    @pl.when(last)
    def _():
        o = acc_scr[...] / d_scr[...]
        for h in range(A_HEADS):
            out_ref[:, h * hd:(h + 1) * hd] = o[h * l_new:(h + 1) * l_new]


def attn_sample(proj, caches, li, prev):
    bn, l_new, _ = proj.shape
    hd = A_HEAD_DIM
    shift = l_new * KV_ROWS

    def col(cb):
        return pl.BlockSpec((None, l_new, A_WIDTH), lambda b, c, cb=cb: (b, 0, cb))

    in_specs = [col(3 * g + j) for g in range(3) for j in range(3)]
    args = [proj] * 9
    chunk_specs, next_specs, out_specs, out_shape = [], [], [], []
    for g in range(3):
        rows = caches[g].shape[2] // SAMPLE_CHUNKS
        assert rows % shift == 0 and rows > shift
        per = rows // shift
        n_blk = caches[g].shape[2] // shift
        chunk_specs.append(pl.BlockSpec((None, None, rows, hd), lambda b, c: (li, b, c, 0)))
        next_specs.append(pl.BlockSpec(
            (None, None, shift, hd),
            lambda b, c, per=per, n_blk=n_blk: (li, b, jnp.minimum((c + 1) * per, n_blk - 1), 0)))
        out_specs.append(pl.BlockSpec((None, None, rows, hd), lambda b, c: (li, b, c, 0)))
        out_shape.append(jax.ShapeDtypeStruct(caches[g].shape, F32))
    in_specs += chunk_specs + next_specs
    args += list(caches) + list(caches)
    aliases = {}
    n_prev = 0
    if prev is not None:
        n_prev = 3
        for g in range(3):
            aliases[len(args)] = 1 + g
            in_specs.append(pl.BlockSpec(memory_space=pl.ANY))
            args.append(prev[g])
    res = pl.pallas_call(
        functools.partial(_attn_sample_kernel, n_prev=n_prev),
        grid=(bn, SAMPLE_CHUNKS),
        in_specs=in_specs,
        out_specs=[pl.BlockSpec((None, l_new, A_WIDTH), lambda b, c: (b, 0, 0))] + out_specs,
        out_shape=[jax.ShapeDtypeStruct((bn, l_new, A_WIDTH), F32)] + out_shape,
        scratch_shapes=[pltpu.VMEM((A_HEADS * l_new, LANES), F32)] * 3 + [pltpu.VMEM((3, shift, hd), F32)],
        input_output_aliases=aliases,
        compiler_params=_cparams(("parallel", "arbitrary"), 48),
        name="attn_sample",
    )(*args)
    return res[0], list(res[1:])


def _kv_pack_kernel(*refs):
    k_ref, v_ref, o_ref = refs[0], refs[1], refs[-1]
    tm = k_ref.shape[0]
    hd = A_HEAD_DIM
    for kv, src in enumerate((k_ref, v_ref)):
        for h in range(A_HEADS):
            o_ref[pl.ds(kv * A_HEADS + h, tm, stride=KV_ROWS), :] = src[:, h * hd:(h + 1) * hd]


def kv_pack(proj, g, li, n_ab, prev):
    bn, t_len, _ = proj.shape
    hd = A_HEAD_DIM
    w = min(A_GROUPS[g][0], t_len)
    tm = min(w, 128)
    row0 = (t_len - w) // tm
    in_specs = [pl.BlockSpec((None, tm, A_WIDTH), lambda b, i: (b, row0 + i, 3 * g + 1)),
                pl.BlockSpec((None, tm, A_WIDTH), lambda b, i: (b, row0 + i, 3 * g + 2))]
    args = [proj, proj]
    aliases = {}
    if prev is not None:
        aliases[2] = 0
        in_specs.append(pl.BlockSpec(memory_space=pl.ANY))
        args.append(prev)
    return pl.pallas_call(
        _kv_pack_kernel,
        grid=(bn, w // tm),
        in_specs=in_specs,
        out_specs=pl.BlockSpec((None, None, tm * KV_ROWS, hd), lambda b, i: (li, b, i, 0)),
        out_shape=jax.ShapeDtypeStruct((n_ab, bn, w * KV_ROWS, hd), F32),
        input_output_aliases=aliases,
        compiler_params=_cparams(("parallel", "parallel"), 32),
        name="kv_pack",
    )(*args)


def _gla_kernel(bq_ref, bf_ref, bi_ref, bg_ref, lbp_ref, gain_ref, s0_ref,
                o_ref, sfin_ref, st_scr, att_scr, qkb_scr, *, li, cin):
    c = pl.program_id(2)
    ch = B_CHUNK
    last = c == pl.num_programs(2) - 1

    def pad(x):
        if cin == ch:
            return x
        return jnp.concatenate([x, jnp.zeros((ch - cin, x.shape[1]), x.dtype)], axis=0)

    row = lax.broadcasted_iota(jnp.int32, (ch, B_DIM), 0)
    lane = lax.broadcasted_iota(jnp.int32, (B_SUB, ch), 1)
    row_sub = lax.broadcasted_iota(jnp.int32, (B_SUB, B_DIM), 0)
    live = row < cin
    tri = (row >= lax.broadcasted_iota(jnp.int32, (ch, ch), 1)).astype(F32)

    @pl.when(c == 0)
    def _():
        for hj in range(B_HEADS_PER_STEP):
            st_scr[hj] = s0_ref[hj].T

    for hj in range(B_HEADS_PER_STEP):
        hs = slice(hj * B_DIM, (hj + 1) * B_DIM)
        lbp = lbp_ref[:, hs]
        e = jnp.exp(lbp - jnp.max(lbp, axis=0, keepdims=True))
        soft = e / jnp.sum(e, axis=0, keepdims=True)
        lb = jnp.sum(soft[0:li + 1], axis=0, keepdims=True) - soft[0:1]

        q = _silu(pad(bq_ref[:, hs]))
        fgate = lb + (1.0 - lb) * _sigmoid(pad(bf_ref[:, hs]))
        g = jnp.where(live, jnp.log(fgate), 0.0)
        kk = jnp.where(live, 1.0 - fgate, 0.0)
        v16 = pad(bi_ref[:, hs]).astype(BF16)

        b = jnp.dot(tri, g, preferred_element_type=F32, precision=lax.Precision.HIGHEST)
        b_last = b[ch - 1:ch]
        st = st_scr[hj]

        o = _dot_nt((q * jnp.exp(b)).astype(BF16), st.astype(BF16))

        q_scr, kk_scr, b_scr = (qkb_scr.at[hj, j] for j in range(3))
        q_scr[...] = q
        kk_scr[...] = kk
        b_scr[...] = b

        for qb in range(ch // B_SUB):
            r0 = qb * B_SUB
            q_s = q_scr[r0:r0 + B_SUB, :]
            b_s = b_scr[r0:r0 + B_SUB, :]
            kk_s = kk_scr[r0:r0 + B_SUB, :]
            if qb > 0:
                b_ref = b_scr[r0 - 1:r0, :]
                qf = q_s * jnp.exp(b_s - b_ref)
                earlier = row < r0
                kf = jnp.where(earlier, kk_scr[...] * jnp.exp(jnp.where(earlier, b_ref - b_scr[...], 0.0)), 0.0)
                att = _dot_nt(qf.astype(BF16), kf.astype(BF16))
            else:
                att = jnp.zeros((B_SUB, ch), F32)
            for s in range(B_SUB):
                causal = row_sub >= s
                dec = jnp.where(causal, jnp.exp(jnp.where(causal, b_s - b_s[s:s + 1], 0.0)), 0.0)
                colv = jnp.sum(q_s * dec * kk_s[s:s + 1], axis=-1, keepdims=True)
                att = jnp.where(lane == r0 + s, colv, att)
            att_scr[hj, r0:r0 + B_SUB, :] = att

        o = o + _dot(att_scr[hj].astype(BF16), v16)

        kd = kk * jnp.exp(b_last - b)
        st_new = st * jnp.exp(b_last) + _dot_tn(v16, kd.astype(BF16))
        st_scr[hj] = st_new

        on = _rms(o, gain_ref[:, hs]) * _silu(pad(bg_ref[:, hs]))
        o_ref[:, hs] = on[0:cin].astype(o_ref.dtype)

    @pl.when(last)
    def _():
        for hj in range(B_HEADS_PER_STEP):
            sfin_ref[hj] = st_scr[hj].T


def gla(proj, lb_params, b_gain, s0, li):
    bn, t_len, ab_in = proj.shape
    n_ab = lb_params.shape[0]
    cin = min(B_CHUNK, t_len)
    nc = t_len // cin
    base = ab_in // B_DIM - 4 * B_HEADS

    hp = B_HEADS_PER_STEP
    wid = hp * B_DIM
    base = base // hp

    def col(j):
        return pl.BlockSpec((None, cin, wid), lambda b, h, c, j=j: (b, c, base + j * (B_HEADS // hp) + h))

    out_dtype = F32 if cin < 16 else BF16
    return pl.pallas_call(
        functools.partial(_gla_kernel, li=li, cin=cin),
        grid=(bn, B_HEADS // hp, nc),
        in_specs=[col(0), col(1), col(2), col(3),
                  pl.BlockSpec((n_ab, wid), lambda b, h, c: (0, h)),
                  pl.BlockSpec((None, 1, wid), lambda b, h, c: (li, 0, h)),
                  pl.BlockSpec((None, hp, B_DIM, B_DIM), lambda b, h, c: (b, h, 0, 0))],
        out_specs=[pl.BlockSpec((None, cin, wid), lambda b, h, c: (b, c, h)),
                   pl.BlockSpec((None, hp, B_DIM, B_DIM), lambda b, h, c: (b, h, 0, 0))],
        out_shape=[jax.ShapeDtypeStruct((bn, t_len, B_HEADS * B_DIM), out_dtype),
                   jax.ShapeDtypeStruct((bn, B_HEADS, B_DIM, B_DIM), F32)],
        scratch_shapes=[pltpu.VMEM((hp, B_DIM, B_DIM), F32), pltpu.VMEM((hp, B_CHUNK, B_CHUNK), F32),
                        pltpu.VMEM((hp, 3, B_CHUNK, B_DIM), F32)],
        compiler_params=_cparams(("parallel", "parallel", "arbitrary"), 32),
        name="gla",
    )(proj, proj, proj, proj, lb_params, b_gain.reshape(n_ab, 1, -1), s0)


def _mix_kernel(x_ref, xp_ref, sh_ref, g_ref, mu_ref, *outs):
    xm_refs, hl_ref = outs[:6], outs[6]
    i = pl.program_id(1)
    tt = x_ref.shape[0]
    g = g_ref[...]
    h = _rms(x_ref[...], g)
    hp = _rms(xp_ref[...], g)
    first = jnp.where(i == 0, sh_ref[...], hp[7:8])
    rolled = pltpu.roll(h, 1, axis=0)
    row = lax.broadcasted_iota(jnp.int32, h.shape, 0)
    prev = jnp.where(row == 0, first, rolled)
    xx = prev - h
    for j in range(6):
        xm_refs[j][...] = (h + xx * mu_ref[j:j + 1]).astype(BF16)

    @pl.when(i == pl.num_programs(1) - 1)
    def _():
        hl_ref[...] = h[tt - 1:tt]


def rwkv_mix(x, shift, g, mu):
    bn, t_len, d = x.shape
    tt = min(t_len, 256)
    sub = tt // 8
    tile = pl.BlockSpec((None, tt, d), lambda b, i: (b, i, 0))
    *xm, hl = pl.pallas_call(
        _mix_kernel,
        grid=(bn, t_len // tt),
        in_specs=[pl.BlockSpec((None, tt, d), lambda b, i: (b, i, 0)),
                  pl.BlockSpec((None, 8, d), lambda b, i: (b, jnp.maximum(i * sub - 1, 0), 0)),
                  pl.BlockSpec((None, 1, d), lambda b, i: (b, 0, 0)),
                  pl.BlockSpec((1, d), lambda b, i: (0, 0)),
                  pl.BlockSpec((6, d), lambda b, i: (0, 0))],
        out_specs=[tile] * 6 + [pl.BlockSpec((None, 1, d), lambda b, i: (b, 0, 0))],
        out_shape=[jax.ShapeDtypeStruct((bn, t_len, d), BF16)] * 6 + [jax.ShapeDtypeStruct((bn, 1, d), F32)],
        compiler_params=_cparams(("parallel", "arbitrary"), 40),
        name="rwkv_mix",
    )(x, x, shift.reshape(bn, 1, d), g.reshape(1, d), mu)
    return [t.reshape(bn * t_len, d) for t in xm], hl.reshape(bn, d)


def _scan_kernel(r_ref, k_ref, v_ref, wp_ref, ap_ref, par_ref, s0_ref,
                 o_ref, sfin_ref, st, ops, *, tc):
    c = pl.program_id(1)
    n = C_HEAD_DIM

    @pl.when(c == 0)
    def _():
        st[...] = s0_ref[...]

    def key_row(j, ki):
        return ops[j, pl.ds(ki, 1), :]

    def step(t, carry):
        c_kk, c_ka, c_rk, ln_w, ln_b = (par_ref[j] for j in range(5))
        z = -wp_ref[t]
        softplus = jnp.maximum(z, 0.0) + jnp.log(1.0 + jnp.exp(-jnp.abs(z)))
        w = jnp.exp(-jnp.exp(-softplus - 0.5))
        a = _sigmoid(ap_ref[t])
        kt = k_ref[t]
        kk = kt * c_kk
        nrm = jnp.sqrt(jnp.sum(kk * kk, axis=0, keepdims=True))
        kk = kk / jnp.maximum(nrm, 1e-12)
        k2 = kt * (1.0 + (a - 1.0) * c_ka)
        rt = r_ref[t]
        vt = v_ref[t]
        ops[0] = -kk
        ops[1] = w
        ops[2] = kk * a
        ops[3] = k2
        ops[4] = rt
        sa0 = st[0] * key_row(0, 0)
        sa1 = st[1] * key_row(0, 1)
        for ki in range(2, n, 2):
            sa0 = sa0 + st[ki] * key_row(0, ki)
            sa1 = sa1 + st[ki + 1] * key_row(0, ki + 1)
        sa = sa0 + sa1
        o0 = o1 = None
        for ki in range(n):
            s_n = st[ki] * key_row(1, ki) + sa * key_row(2, ki) + vt * key_row(3, ki)
            st[ki] = s_n
            term = s_n * key_row(4, ki)
            if ki % 2 == 0:
                o0 = term if o0 is None else o0 + term
            else:
                o1 = term if o1 is None else o1 + term
        o = o0 + o1
        mean = jnp.mean(o, axis=0, keepdims=True)
        var = jnp.mean(jnp.square(o - mean), axis=0, keepdims=True)
        o = (o - mean) * lax.rsqrt(var + C_GN_EPS) * ln_w + ln_b
        bonus = jnp.sum(rt * k2 * c_rk, axis=0, keepdims=True)
        o_ref[t] = o + bonus * vt
        return carry

    lax.fori_loop(0, tc, step, 0)

    @pl.when(c == pl.num_programs(1) - 1)
    def _():
        sfin_ref[...] = st[...]


def rwkv_scan(r, k, v, wp, ap, par, s0):
    t_len, n, bh = r.shape
    tc = min(t_len, 32)
    seq = pl.BlockSpec((tc, n, LANES), lambda l, c: (c, 0, l))
    return pl.pallas_call(
        functools.partial(_scan_kernel, tc=tc),
        grid=(bh // LANES, t_len // tc),
        in_specs=[seq] * 5 + [pl.BlockSpec((5, n, LANES), lambda l, c: (0, 0, l)),
                              pl.BlockSpec((n, n, LANES), lambda l, c: (0, 0, l))],
        out_specs=[seq, pl.BlockSpec((n, n, LANES), lambda l, c: (0, 0, l))],
        out_shape=[jax.ShapeDtypeStruct((t_len, n, bh), F32),
                   jax.ShapeDtypeStruct((n, n, bh), F32)],
        scratch_shapes=[pltpu.VMEM((n, n, LANES), F32), pltpu.VMEM((5, n, LANES), F32)],
        compiler_params=_cparams(("parallel", "arbitrary"), 40),
        name="rwkv_scan",
    )(r, k, v, wp, ap, par, s0)


def _tiles_time(t_len):
    return t_len % 256 == 0


def _pad_to(x, axis, mult):
    pad = (-x.shape[axis]) % mult
    if pad == 0:
        return x
    widths = [(0, 0)] * x.ndim
    widths[axis] = (0, pad)
    return jnp.pad(x, widths)


def _ab_layer(x, h, li, p, caches, b_state, prev_bufs):
    bn, t_len, d = x.shape
    n_ab = p["w_in_ab"].shape[0]
    proj = matmul(h, p["w_in_ab"][li]).reshape(bn, t_len, -1)
    if caches is None:
        a_out = attn_prompt(proj)
        new_bufs = [kv_pack(proj, g, li, n_ab, None if prev_bufs is None else prev_bufs[g]) for g in range(3)]
        s0 = jnp.zeros((bn, B_HEADS, B_DIM, B_DIM), F32)
    else:
        a_out, new_bufs = attn_sample(proj, caches, li, prev_bufs)
        a_out = a_out.astype(BF16)
        s0 = b_state[li]
    b_out, s_b = gla(proj, p["b_lower_bounds"], p["b_norm_gain"], s0, li)
    mix_in = jnp.concatenate([a_out, b_out.astype(BF16)], axis=-1).reshape(bn * t_len, -1)
    return mix_in, new_bufs, s_b


def _rwkv_layer(x, li, g_pre, p, shift, wkv):
    bn, t_len, d = x.shape
    n = C_HEAD_DIM
    hh = d // n
    (xr, xw, xk, xv, xa, xg), h_last = rwkv_mix(x, shift, g_pre, p["c_mu"][li])

    def proj_tm(a, w, bias=None):
        if _tiles_time(t_len):
            return matmul(a, w, bias=bias, tb=(bn, t_len))
        y = matmul(a, w, bias=bias)
        return jnp.swapaxes(y.reshape(bn, t_len, -1), 1, 2).reshape(-1, t_len)

    w_rkv = p["c_w_rkv"][li]
    r = proj_tm(xr, w_rkv[0])
    k = proj_tm(xk, w_rkv[1])
    v = proj_tm(xv, w_rkv[2])
    w_lo = matmul(xw, _pad_to(p["c_w1"][li], 1, LANES), act="tanh", out_dtype=BF16)
    wp = proj_tm(w_lo, _pad_to(p["c_w2"][li], 0, LANES), bias=p["c_w0"][li])
    a_lo = matmul(xa, _pad_to(p["c_a1"][li], 1, LANES), out_dtype=BF16)
    ap = proj_tm(a_lo, _pad_to(p["c_a2"][li], 0, LANES), bias=p["c_a0"][li])
    g_lo = matmul(xg, _pad_to(p["c_g1"][li], 1, LANES), act="sigmoid", out_dtype=BF16)
    gate = proj_tm(g_lo, _pad_to(p["c_g2"][li], 0, LANES))

    def to_scan(t):
        return jnp.transpose(t.reshape(bn * hh, n, t_len), (2, 1, 0))

    def lanes(vec):
        return jnp.tile(vec.reshape(hh, n).T, (1, bn))

    par = jnp.stack([lanes(p["c_k_k"][li]), lanes(p["c_k_a"][li]), lanes(p["c_r_k"][li].reshape(-1)),
                     lanes(p["c_ln_w"][li]), lanes(p["c_ln_b"][li])])
    s0 = jnp.transpose(wkv, (3, 2, 0, 1)).reshape(n, n, bn * hh)
    o, s_fin = rwkv_scan(*(to_scan(t) for t in (r, k, v, wp, ap)), par, s0)
    o = jnp.transpose(o, (2, 1, 0)).reshape(bn * d, t_len)
    s_fin = jnp.transpose(s_fin.reshape(n, n, bn, hh), (2, 3, 1, 0))
    return o, gate, s_fin, h_last


def _trunk(x, caches, b_states, c_wkv, c_shift, p):
    bn, t_len, d = x.shape
    m = bn * t_len
    depth = p["norm_gains"].shape[0]
    kv_bufs = None
    if caches is not None:
        caches = [c.reshape(c.shape[0], c.shape[1], -1, c.shape[-1]) for c in caches]
    new_b, new_wkv, new_shift = [], [], []
    h = rmsnorm_rows(x.reshape(m, d), p["norm_gains"][0, 0])
    for layer in range(depth):
        gains = p["norm_gains"][layer]
        li = layer // 2
        last = layer == depth - 1
        next_pre = p["norm_gains"][layer + 1, 0] if not last else gains[0]
        if layer % 2 == 0:
            mix_in, kv_bufs, s_b = _ab_layer(x, h, li, p, caches, b_states, kv_bufs)
            new_b.append(s_b)
            x2, h2 = matmul_post(mix_in, p["w_out_ab"][li], gains[1], x.reshape(m, d), gains[2])
        else:
            wkv0 = jnp.zeros((bn, d // C_HEAD_DIM, C_HEAD_DIM, C_HEAD_DIM), F32) if c_wkv is None else c_wkv[li]
            sh0 = jnp.zeros((bn, d), F32) if c_shift is None else c_shift[li]
            o, gate, s_wkv, s_shift = _rwkv_layer(x, li, gains[0], p, sh0, wkv0)
            new_wkv.append(s_wkv)
            new_shift.append(s_shift)
            if _tiles_time(t_len):
                tb = (bn, t_len)
            else:
                tb = None
                o, gate = (jnp.swapaxes(t.reshape(bn, d, t_len), 1, 2).reshape(m, d) for t in (o, gate))
            x2, h2 = matmul_post(o, p["c_w_out"][li], gains[1], x.reshape(m, d), gains[2], gate=gate, tb=tb)
        next_even = (layer + 1) % 2 == 0 and not last
        x3, h = ffn(h2, p["w_ffn_up"][layer], p["w_ffn_down"][layer], gains[3], x2, next_pre,
                    emit_h=next_even)
        x = x3.reshape(bn, t_len, d)
    a_states = tuple(kv.reshape(kv.shape[0], bn, -1, 2, A_HEADS, A_HEAD_DIM) for kv in kv_bufs)
    return x, a_states, jnp.stack(new_b), jnp.stack(new_wkv), jnp.stack(new_shift)


def kernel(x_prompt, x_sample, cache_a1_kv, cache_a2_kv, cache_a3_kv, state_b, state_c_wkv, state_c_shift, norm_gains, w_in_ab, w_out_ab, b_lower_bounds, b_norm_gain, c_mu, c_w_rkv, c_w0, c_w1, c_w2, c_a0, c_a1, c_a2, c_g1, c_g2, c_k_k, c_k_a, c_r_k, c_ln_w, c_ln_b, c_w_out, w_ffn_up, w_ffn_down):
    to16 = lambda w: w.astype(BF16)
    p = dict(norm_gains=norm_gains, w_in_ab=to16(w_in_ab), w_out_ab=to16(w_out_ab),
             b_lower_bounds=b_lower_bounds, b_norm_gain=b_norm_gain, c_mu=c_mu,
             c_w_rkv=to16(c_w_rkv), c_w0=c_w0, c_w1=to16(c_w1), c_w2=to16(c_w2),
             c_a0=c_a0, c_a1=to16(c_a1), c_a2=to16(c_a2), c_g1=to16(c_g1), c_g2=to16(c_g2),
             c_k_k=c_k_k, c_k_a=c_k_a, c_r_k=c_r_k, c_ln_w=c_ln_w, c_ln_b=c_ln_b,
             c_w_out=to16(c_w_out), w_ffn_up=to16(w_ffn_up), w_ffn_down=to16(w_ffn_down))
    y_p, (pa1, pa2, pa3), pb, pwkv, pshift = _trunk(x_prompt, None, None, None, None, p)
    y_s, (sa1, sa2, sa3), sb, swkv, sshift = _trunk(
        x_sample, (cache_a1_kv, cache_a2_kv, cache_a3_kv), state_b, state_c_wkv, state_c_shift, p)
    return (y_p, y_s, pa1, pa2, pa3, pb, pwkv, pshift, sa1, sa2, sa3, sb, swkv, sshift)
```

```python
import functools

import jax
import jax.numpy as jnp
from jax import lax
from jax.experimental import pallas as pl
from jax.experimental.pallas import tpu as pltpu

F32 = jnp.float32
BF16 = jnp.bfloat16

NORM_EPS = 1e-6
A_GROUPS = ((128, 1), (512, 4), (2048, 16))
A_HEADS = 8
A_HEAD_DIM = 128
A_WIDTH = A_HEADS * A_HEAD_DIM
A_BLK = 128
B_HEADS = 8
B_DIM = 128
B_CHUNK = 128
B_SUB = 8
B_HEADS_PER_STEP = 2
C_HEAD_DIM = 64
C_GN_EPS = 64e-5
LANES = 128
MIB = 1024 * 1024


def _cparams(sem, vmem_mib):
    return pltpu.CompilerParams(dimension_semantics=sem, vmem_limit_bytes=vmem_mib * MIB)


def _rms(x, g):
    return x * lax.rsqrt(jnp.mean(x * x, axis=-1, keepdims=True) + NORM_EPS) * g


def _dot(a, b):
    return jnp.dot(a, b, preferred_element_type=F32)


def _dot_nt(a, b):
    return lax.dot_general(a, b, (((1,), (1,)), ((), ())), preferred_element_type=F32)


def _dot_tn(a, b):
    return lax.dot_general(a, b, (((0,), (0,)), ((), ())), preferred_element_type=F32)


def _log2(n):
    assert n > 0 and n & (n - 1) == 0, n
    return n.bit_length() - 1


def _sigmoid(x):
    return 1.0 / (1.0 + jnp.exp(-x))


def _silu(x):
    return x * _sigmoid(x)


def _norm_kernel(x_ref, g_ref, o_ref):
    o_ref[...] = _rms(x_ref[...], g_ref[...]).astype(o_ref.dtype)


def rmsnorm_rows(x, g):
    m, d = x.shape
    tm = min(m, 512)
    return pl.pallas_call(
        _norm_kernel,
        grid=(m // tm,),
        in_specs=[pl.BlockSpec((tm, d), lambda i: (i, 0)),
                  pl.BlockSpec((1, d), lambda i: (0, 0))],
        out_specs=pl.BlockSpec((tm, d), lambda i: (i, 0)),
        out_shape=jax.ShapeDtypeStruct((m, d), BF16),
        compiler_params=_cparams(("parallel",), 32),
        name="rmsnorm_rows",
    )(x, g.reshape(1, d))


def _rwkv_decay(z):
    y = -z
    softplus = jnp.maximum(y, 0.0) + jnp.log(1.0 + jnp.exp(-jnp.abs(y)))
    return jnp.exp(-jnp.exp(-softplus - 0.5))


def _mm_kernel(*refs, act, has_bias, w_t, out_t):
    if has_bias:
        a_ref, w_ref, b_ref, o_ref = refs
    else:
        a_ref, w_ref, o_ref = refs
    w = w_ref[...].astype(BF16)
    if out_t:
        y = _dot_nt(w, a_ref[...])
    elif w_t:
        y = _dot_nt(a_ref[...], w)
    else:
        y = _dot(a_ref[...], w)
    if has_bias:
        y = y + b_ref[...]
    if act == "tanh":
        y = jnp.tanh(y)
    elif act == "sigmoid":
        y = _sigmoid(y)
    elif act == "rwkv_decay":
        y = _rwkv_decay(y)
    o_ref[...] = y.astype(o_ref.dtype)


def matmul(a, w, widx=(), bias=None, act=None, out_dtype=F32, tb=None, w_t=False):
    m, k = a.shape
    n = w.shape[-2] if w_t else w.shape[-1]
    assert (w.shape[-1] if w_t else w.shape[-2]) == k and (tb is None or w_t)
    tn = min(n, 1024 if w.dtype == BF16 else 512)
    nj = n // tn
    if tb is None:
        tm = min(m, 1024)
        out_shape = (m, n)
        out_block = (tm, tn)
        out_index = lambda i, j: (i, j)
    else:
        bn, t_len = tb
        tm = min(t_len, 1024)
        nt = t_len // tm
        out_shape = (bn * n, t_len)
        out_block = (tn, tm)
        out_index = lambda i, j: ((i // nt) * nj + j, i % nt)
    lead = (None,) * len(widx)
    in_specs = [pl.BlockSpec((tm, k), lambda i, j: (i, 0))]
    if w_t:
        in_specs.append(pl.BlockSpec(lead + (tn, k), lambda i, j: widx + (j, 0)))
    else:
        in_specs.append(pl.BlockSpec(lead + (k, tn), lambda i, j: widx + (0, j)))
    args = [a, w]
    if bias is not None:
        if tb is None:
            in_specs.append(pl.BlockSpec((1, tn), lambda i, j: (0, j)))
            args.append(bias.reshape(1, n))
        else:
            in_specs.append(pl.BlockSpec((tn, 1), lambda i, j: (j, 0)))
            args.append(bias.reshape(n, 1))
    return pl.pallas_call(
        functools.partial(_mm_kernel, act=act, has_bias=bias is not None, w_t=w_t, out_t=tb is not None),
        grid=(m // tm, nj),
        in_specs=in_specs,
        out_specs=pl.BlockSpec(out_block, out_index),
        out_shape=jax.ShapeDtypeStruct(out_shape, out_dtype),
        compiler_params=_cparams(("parallel", "parallel"), 40),
        name="matmul",
    )(*args)


def _mm_post_kernel(*refs, has_gate, emit_h, transpose_in):
    refs = list(refs)
    a_ref = refs.pop(0)
    gate_ref = refs.pop(0) if has_gate else None
    w_ref, g1_ref, xres_ref, gn_ref, x_out = refs[:5]
    a = a_ref[...]
    if has_gate:
        a = a * gate_ref[...]
    a = a.astype(BF16)
    y = _dot_tn(a, w_ref[...]) if transpose_in else _dot(a, w_ref[...])
    x_new = xres_ref[...] + _rms(y, g1_ref[...])
    x_out[...] = x_new
    if emit_h:
        refs[5][...] = _rms(x_new, gn_ref[...]).astype(BF16)


def matmul_post(a, w, li, g_post, xres, g_next, gate=None, emit_h=True, tb=None):
    m, d = xres.shape
    k = w.shape[1]
    tm = min(m, 256) if tb is None else min(tb[1], 256)
    row = lambda i: (i, 0)
    fixed = lambda i: (0, 0)
    if tb is None:
        a_spec = pl.BlockSpec((tm, k), row)
    else:
        nt = tb[1] // tm
        a_spec = pl.BlockSpec((k, tm), lambda i: (i // nt, i % nt))
    in_specs = [a_spec]
    args = [a]
    if gate is not None:
        in_specs.append(a_spec)
        args.append(gate)
    in_specs += [pl.BlockSpec((None, k, d), lambda i: (li, 0, 0)), pl.BlockSpec((1, d), fixed),
                 pl.BlockSpec((tm, d), row), pl.BlockSpec((1, d), fixed)]
    args += [w, g_post.reshape(1, d), xres, g_next.reshape(1, d)]
    out_specs = [pl.BlockSpec((tm, d), row)]
    out_shape = [jax.ShapeDtypeStruct((m, d), F32)]
    if emit_h:
        out_specs.append(pl.BlockSpec((tm, d), row))
        out_shape.append(jax.ShapeDtypeStruct((m, d), BF16))
    res = pl.pallas_call(
        functools.partial(_mm_post_kernel, has_gate=gate is not None, emit_h=emit_h, transpose_in=tb is not None),
        grid=(m // tm,),
        in_specs=in_specs,
        out_specs=out_specs,
        out_shape=out_shape,
        compiler_params=_cparams(("parallel",), 48),
        name="matmul_post",
    )(*args)
    return (res[0], res[1]) if emit_h else (res[0], None)


def _ffn_kernel(h_ref, wg_ref, wu_ref, wd_ref, g1_ref, xres_ref, gn_ref, *outs, emit_h):
    if emit_h:
        x_out, h_out, acc_ref = outs
    else:
        x_out, acc_ref = outs
    j = pl.program_id(1)

    @pl.when(j == 0)
    def _():
        acc_ref[...] = jnp.zeros_like(acc_ref)

    h = h_ref[...]
    gate = _dot(h, wg_ref[...])
    up = _dot(h, wu_ref[...])
    act = (_silu(gate) * up).astype(BF16)
    acc_ref[...] += _dot(act, wd_ref[...])

    @pl.when(j == pl.num_programs(1) - 1)
    def _():
        x_new = xres_ref[...] + _rms(acc_ref[...], g1_ref[...])
        x_out[...] = x_new
        if emit_h:
            h_out[...] = _rms(x_new, gn_ref[...]).astype(BF16)


def ffn(h, w_up, w_down, layer, g_post, xres, g_next, emit_h=True):
    m, d = h.shape
    f = w_down.shape[1]
    tm = min(m, 512)
    th = 512
    nj = f // th
    row = lambda i, j: (i, 0)
    fixed = lambda i, j: (0, 0)
    in_specs = [pl.BlockSpec((tm, d), row),
                pl.BlockSpec((None, d, th), lambda i, j: (layer, 0, j)),
                pl.BlockSpec((None, d, th), lambda i, j: (layer, 0, j + nj)),
                pl.BlockSpec((None, th, d), lambda i, j: (layer, j, 0)),
                pl.BlockSpec((1, d), fixed),
                pl.BlockSpec((tm, d), row),
                pl.BlockSpec((1, d), fixed)]
    out_specs = [pl.BlockSpec((tm, d), row)]
    out_shape = [jax.ShapeDtypeStruct((m, d), F32)]
    if emit_h:
        out_specs.append(pl.BlockSpec((tm, d), row))
        out_shape.append(jax.ShapeDtypeStruct((m, d), BF16))
    res = pl.pallas_call(
        functools.partial(_ffn_kernel, emit_h=emit_h),
        grid=(m // tm, nj),
        in_specs=in_specs,
        out_specs=out_specs,
        out_shape=out_shape,
        scratch_shapes=[pltpu.VMEM((tm, d), F32)],
        compiler_params=_cparams(("parallel", "arbitrary"), 52),
        name="ffn",
    )(h, w_up, w_up, w_down, g_post.reshape(1, d), xres, g_next.reshape(1, d))
    return (res[0], res[1]) if emit_h else (res[0], None)


def _alibi_slope(shape):
    h = pl.program_id(1)
    expo = (h + 1).astype(F32) * (-8.0 / A_HEADS)
    return jnp.exp2(jnp.full(shape, 1.0, F32) * expo)


def _attn_prompt_kernel(q1, k1, v1, q2, k2, v2, q3, k3, v3, out_ref, o_scr, l_scr):
    t_len = out_ref.shape[0]
    blk = A_BLK
    scale = A_HEAD_DIM ** -0.5
    slope = _alibi_slope((blk, blk))
    qi = lax.broadcasted_iota(jnp.int32, (blk, blk), 0)
    ki = lax.broadcasted_iota(jnp.int32, (blk, blk), 1)
    dist_cur = (qi - ki).astype(F32)
    dist_prev = (qi - ki + blk).astype(F32)
    mask_cur = ki <= qi
    mask_prev = ki >= qi
    neg_inf = jnp.float32(-jnp.inf)

    def rows(start, dil):
        if dil == 1:
            return pl.ds(pl.multiple_of(start, blk), blk)
        return pl.ds(start, blk, stride=dil)

    def one_block(gi, q_ref, k_ref, v_ref, dil, start, prev_start, use_prev):
        q = q_ref[rows(start, dil), :].astype(BF16)
        kc = k_ref[rows(start, dil), :].astype(BF16)
        vc = v_ref[rows(start, dil), :].astype(BF16)
        s_c = _dot_nt(q, kc) * scale - slope * (dist_cur * float(dil))
        s_c = jnp.where(mask_cur, s_c, neg_inf)
        mx = jnp.max(s_c, axis=-1, keepdims=True)
        if use_prev is not None:
            kp = k_ref[rows(prev_start, dil), :].astype(BF16)
            vp = v_ref[rows(prev_start, dil), :].astype(BF16)
            s_p = _dot_nt(q, kp) * scale - slope * (dist_prev * float(dil))
            s_p = jnp.where(mask_prev, s_p + jnp.where(use_prev, 0.0, neg_inf), neg_inf)
            mx = jnp.maximum(mx, jnp.max(s_p, axis=-1, keepdims=True))
        p_c = jnp.exp(s_c - mx)
        den = jnp.sum(p_c, axis=-1, keepdims=True)
        acc = _dot(p_c.astype(BF16), vc)
        if use_prev is not None:
            p_p = jnp.exp(s_p - mx)
            den = den + jnp.sum(p_p, axis=-1, keepdims=True)
            acc = acc + _dot(p_p.astype(BF16), vp)
        o = acc / den
        lse = mx + jnp.log(den)
        o_scr[gi, rows(start, dil), :] = o
        l_scr[gi, rows(start, dil), :] = jnp.broadcast_to(lse, (blk, LANES))

    for gi, (q_ref, k_ref, v_ref) in enumerate(((q1, k1, v1), (q2, k2, v2), (q3, k3, v3))):
        window, dil = A_GROUPS[gi]
        assert window // dil == blk
        m_len = t_len // dil
        nb = m_len // blk
        assert nb * blk * dil == t_len

        def body(idx, carry, gi=gi, q_ref=q_ref, k_ref=k_ref, v_ref=v_ref, dil=dil, nb=nb):
            r = idx // nb
            n = idx - r * nb
            start = n * (blk * dil) + r
            if nb == 1:
                one_block(gi, q_ref, k_ref, v_ref, dil, start, None, None)
            else:
                prev_start = jnp.maximum(n - 1, 0) * (blk * dil) + r
                one_block(gi, q_ref, k_ref, v_ref, dil, start, prev_start, n > 0)
            return carry

        lax.fori_loop(0, dil * nb, body, 0, unroll=16)

    def merge(c, carry):
        sl = pl.ds(pl.multiple_of(c * blk, blk), blk)
        l0, l1, l2 = l_scr[0, sl, :], l_scr[1, sl, :], l_scr[2, sl, :]
        mx = jnp.maximum(jnp.maximum(l0, l1), l2)
        w0, w1, w2 = jnp.exp(l0 - mx), jnp.exp(l1 - mx), jnp.exp(l2 - mx)
        num = w0 * o_scr[0, sl, :] + w1 * o_scr[1, sl, :] + w2 * o_scr[2, sl, :]
        out_ref[sl, :] = (num / (w0 + w1 + w2)).astype(out_ref.dtype)
        return carry

    lax.fori_loop(0, t_len // blk, merge, 0, unroll=2)


def attn_prompt(proj):
    bn, t_len, _ = proj.shape
    hd = A_HEAD_DIM

    def col(cb):
        return pl.BlockSpec((None, t_len, hd), lambda b, h, cb=cb: (b, 0, cb * A_HEADS + h))

    in_specs = [col(3 * g + j) for g in range(3) for j in range(3)]
    return pl.pallas_call(
        _attn_prompt_kernel,
        grid=(bn, A_HEADS),
        in_specs=in_specs,
        out_specs=pl.BlockSpec((None, t_len, hd), lambda b, h: (b, 0, h)),
        out_shape=jax.ShapeDtypeStruct((bn, t_len, A_WIDTH), BF16),
        scratch_shapes=[pltpu.VMEM((3, t_len, hd), F32), pltpu.VMEM((3, t_len, LANES), F32)],
        compiler_params=_cparams(("parallel", "parallel"), 40),
        name="attn_prompt",
    )(*([proj] * 9))


KV_ROWS = 2 * A_HEADS
SAMPLE_CHUNKS = 4


def _attn_sample_kernel(*refs, n_prev):
    cols = refs[0:9]
    chunk_refs = refs[9:12]
    next_refs = refs[12:15]
    out_ref = refs[15 + n_prev]
    new_refs = refs[16 + n_prev:19 + n_prev]
    m_scr, d_scr, acc_scr, new_scr = refs[19 + n_prev:]
    c = pl.program_id(1)
    n_chunks = pl.num_programs(1)
    last = c == n_chunks - 1
    l_new = out_ref.shape[0]
    shift = l_new * KV_ROWS
    hd = A_HEAD_DIM
    scale = hd ** -0.5
    neg_inf = jnp.float32(-jnp.inf)

    nq = A_HEADS * l_new

    @pl.when(c == 0)
    def _():
        m_scr[...] = jnp.full(m_scr.shape, neg_inf, F32)
        d_scr[...] = jnp.zeros(d_scr.shape, F32)
        acc_scr[...] = jnp.zeros(acc_scr.shape, F32)
        for gi in range(3):
            for kv, src in enumerate(cols[3 * gi + 1:3 * gi + 3]):
                for h in range(A_HEADS):
                    new_scr[gi, pl.ds(kv * A_HEADS + h, l_new, stride=KV_ROWS), :] = src[:, h * hd:(h + 1) * hd]

    def segment(q_all, buf, pos0, n_buf, window, dil, live):
        rows = buf.shape[0]
        keys_on_value_rows = jnp.concatenate([jnp.zeros((A_HEADS, hd), F32), buf[0:rows - A_HEADS]], axis=0)
        s = _dot_nt(q_all, keys_on_value_rows.astype(BF16))
        qrow = lax.broadcasted_iota(jnp.int32, (nq, rows), 0)
        col = lax.broadcasted_iota(jnp.int32, (nq, rows), 1)
        head = qrow >> _log2(l_new)
        dist = n_buf + (qrow & (l_new - 1)) - (pos0 + (col >> _log2(KV_ROWS)))
        valid = jnp.logical_and(
            jnp.logical_and((col & (KV_ROWS - 1)) == A_HEADS + head, (dist & (dil - 1)) == 0),
            jnp.logical_and(dist >= 0, dist <= window))
        slope = jnp.exp2((head + 1).astype(F32) * (-8.0 / A_HEADS))
        s = jnp.where(valid, s * scale - slope * dist.astype(F32), neg_inf)
        if live is not None:
            s = jnp.where(live, s, neg_inf)
        m_old = m_scr[...]
        m_new = jnp.maximum(m_old, jnp.max(s, axis=-1, keepdims=True))
        alpha = jnp.exp(m_old - m_new)
        p = jnp.exp(s - m_new[:, 0:1])
        d_scr[...] = alpha * d_scr[...] + jnp.sum(p, axis=-1, keepdims=True)
        acc_scr[...] = alpha * acc_scr[...] + _dot(p.astype(BF16), buf.astype(BF16))
        m_scr[...] = m_new

    for gi in range(3):
        window, dil = A_GROUPS[gi]
        q_ref = cols[3 * gi]
        c_ref, n_ref, o_ref = chunk_refs[gi], next_refs[gi], new_refs[gi]
        rows = c_ref.shape[0]
        p_len = rows // KV_ROWS
        n_buf = p_len * SAMPLE_CHUNKS

        o_ref[0:rows - shift, :] = c_ref[shift:rows, :]
        o_ref[rows - shift:rows, :] = jnp.where(last, new_scr[gi], n_ref[...])

        q_all = jnp.concatenate([q_ref[:, h * hd:(h + 1) * hd] for h in range(A_HEADS)], axis=0).astype(BF16)
        segment(q_all, new_scr[gi], n_buf, n_buf, window, dil, c == 0)
        segment(q_all, c_ref[...], c * p_len, n_buf, window, dil, None)

    @pl.when(last)
    def _():
        o = acc_scr[...] / d_scr[...]
        for h in range(A_HEADS):
            out_ref[:, h * hd:(h + 1) * hd] = o[h * l_new:(h + 1) * l_new]


def attn_sample(proj, caches, li, prev):
    bn, l_new, _ = proj.shape
    hd = A_HEAD_DIM
    shift = l_new * KV_ROWS

    def col(cb):
        return pl.BlockSpec((None, l_new, A_WIDTH), lambda b, c, cb=cb: (b, 0, cb))

    in_specs = [col(3 * g + j) for g in range(3) for j in range(3)]
    args = [proj] * 9
    chunk_specs, next_specs, out_specs, out_shape = [], [], [], []
    for g in range(3):
        rows = caches[g].shape[2] // SAMPLE_CHUNKS
        assert rows % shift == 0 and rows > shift
        per = rows // shift
        n_blk = caches[g].shape[2] // shift
        chunk_specs.append(pl.BlockSpec((None, None, rows, hd), lambda b, c: (li, b, c, 0)))
        next_specs.append(pl.BlockSpec(
            (None, None, shift, hd),
            lambda b, c, per=per, n_blk=n_blk: (li, b, jnp.minimum((c + 1) * per, n_blk - 1), 0)))
        out_specs.append(pl.BlockSpec((None, None, rows, hd), lambda b, c: (li, b, c, 0)))
        out_shape.append(jax.ShapeDtypeStruct(caches[g].shape, F32))
    in_specs += chunk_specs + next_specs
    args += list(caches) + list(caches)
    aliases = {}
    n_prev = 0
    if prev is not None:
        n_prev = 3
        for g in range(3):
            aliases[len(args)] = 1 + g
            in_specs.append(pl.BlockSpec(memory_space=pl.ANY))
            args.append(prev[g])
    res = pl.pallas_call(
        functools.partial(_attn_sample_kernel, n_prev=n_prev),
        grid=(bn, SAMPLE_CHUNKS),
        in_specs=in_specs,
        out_specs=[pl.BlockSpec((None, l_new, A_WIDTH), lambda b, c: (b, 0, 0))] + out_specs,
        out_shape=[jax.ShapeDtypeStruct((bn, l_new, A_WIDTH), F32)] + out_shape,
        scratch_shapes=[pltpu.VMEM((A_HEADS * l_new, LANES), F32)] * 3 + [pltpu.VMEM((3, shift, hd), F32)],
        input_output_aliases=aliases,
        compiler_params=_cparams(("parallel", "arbitrary"), 48),
        name="attn_sample",
    )(*args)
    return res[0], list(res[1:])


def _kv_pack_kernel(*refs):
    k_ref, v_ref, o_ref = refs[0], refs[1], refs[-1]
    tm = k_ref.shape[0]
    hd = A_HEAD_DIM
    for kv, src in enumerate((k_ref, v_ref)):
        for h in range(A_HEADS):
            o_ref[pl.ds(kv * A_HEADS + h, tm, stride=KV_ROWS), :] = src[:, h * hd:(h + 1) * hd]


def kv_pack(proj, g, li, n_ab, prev):
    bn, t_len, _ = proj.shape
    hd = A_HEAD_DIM
    w = min(A_GROUPS[g][0], t_len)
    tm = min(w, 128)
    row0 = (t_len - w) // tm
    in_specs = [pl.BlockSpec((None, tm, A_WIDTH), lambda b, i: (b, row0 + i, 3 * g + 1)),
                pl.BlockSpec((None, tm, A_WIDTH), lambda b, i: (b, row0 + i, 3 * g + 2))]
    args = [proj, proj]
    aliases = {}
    if prev is not None:
        aliases[2] = 0
        in_specs.append(pl.BlockSpec(memory_space=pl.ANY))
        args.append(prev)
    return pl.pallas_call(
        _kv_pack_kernel,
        grid=(bn, w // tm),
        in_specs=in_specs,
        out_specs=pl.BlockSpec((None, None, tm * KV_ROWS, hd), lambda b, i: (li, b, i, 0)),
        out_shape=jax.ShapeDtypeStruct((n_ab, bn, w * KV_ROWS, hd), F32),
        input_output_aliases=aliases,
        compiler_params=_cparams(("parallel", "parallel"), 32),
        name="kv_pack",
    )(*args)


def _gla_kernel(bq_ref, bf_ref, bi_ref, bg_ref, lbp_ref, gain_ref, s0_ref,
                o_ref, sfin_ref, st_scr, att_scr, qkb_scr, *, li, cin):
    c = pl.program_id(2)
    ch = B_CHUNK
    last = c == pl.num_programs(2) - 1

    def pad(x):
        if cin == ch:
            return x
        return jnp.concatenate([x, jnp.zeros((ch - cin, x.shape[1]), x.dtype)], axis=0)

    row = lax.broadcasted_iota(jnp.int32, (ch, B_DIM), 0)
    lane = lax.broadcasted_iota(jnp.int32, (B_SUB, ch), 1)
    row_sub = lax.broadcasted_iota(jnp.int32, (B_SUB, B_DIM), 0)
    live = row < cin
    tri = (row >= lax.broadcasted_iota(jnp.int32, (ch, ch), 1)).astype(F32)

    @pl.when(c == 0)
    def _():
        for hj in range(B_HEADS_PER_STEP):
            st_scr[hj] = s0_ref[hj].T

    for hj in range(B_HEADS_PER_STEP):
        hs = slice(hj * B_DIM, (hj + 1) * B_DIM)
        lbp = lbp_ref[:, hs]
        e = jnp.exp(lbp - jnp.max(lbp, axis=0, keepdims=True))
        soft = e / jnp.sum(e, axis=0, keepdims=True)
        lb = jnp.sum(soft[0:li + 1], axis=0, keepdims=True) - soft[0:1]

        q = _silu(pad(bq_ref[:, hs]))
        fgate = lb + (1.0 - lb) * _sigmoid(pad(bf_ref[:, hs]))
        g = jnp.where(live, jnp.log(fgate), 0.0)
        kk = jnp.where(live, 1.0 - fgate, 0.0)
        v16 = pad(bi_ref[:, hs]).astype(BF16)

        b = jnp.dot(tri, g, preferred_element_type=F32, precision=lax.Precision.HIGHEST)
        b_last = b[ch - 1:ch]
        st = st_scr[hj]

        o = _dot_nt((q * jnp.exp(b)).astype(BF16), st.astype(BF16))

        q_scr, kk_scr, b_scr = (qkb_scr.at[hj, j] for j in range(3))
        q_scr[...] = q
        kk_scr[...] = kk
        b_scr[...] = b

        for qb in range(ch // B_SUB):
            r0 = qb * B_SUB
            q_s = q_scr[r0:r0 + B_SUB, :]
            b_s = b_scr[r0:r0 + B_SUB, :]
            kk_s = kk_scr[r0:r0 + B_SUB, :]
            if qb > 0:
                b_ref = b_scr[r0 - 1:r0, :]
                qf = q_s * jnp.exp(b_s - b_ref)
                earlier = row < r0
                kf = jnp.where(earlier, kk_scr[...] * jnp.exp(jnp.where(earlier, b_ref - b_scr[...], 0.0)), 0.0)
                att = _dot_nt(qf.astype(BF16), kf.astype(BF16))
            else:
                att = jnp.zeros((B_SUB, ch), F32)
            for s in range(B_SUB):
                causal = row_sub >= s
                dec = jnp.where(causal, jnp.exp(jnp.where(causal, b_s - b_s[s:s + 1], 0.0)), 0.0)
                colv = jnp.sum(q_s * dec * kk_s[s:s + 1], axis=-1, keepdims=True)
                att = jnp.where(lane == r0 + s, colv, att)
            att_scr[hj, r0:r0 + B_SUB, :] = att

        o = o + _dot(att_scr[hj].astype(BF16), v16)

        kd = kk * jnp.exp(b_last - b)
        st_new = st * jnp.exp(b_last) + _dot_tn(v16, kd.astype(BF16))
        st_scr[hj] = st_new

        on = _rms(o, gain_ref[:, hs]) * _silu(pad(bg_ref[:, hs]))
        o_ref[:, hs] = on[0:cin].astype(o_ref.dtype)

    @pl.when(last)
    def _():
        for hj in range(B_HEADS_PER_STEP):
            sfin_ref[hj] = st_scr[hj].T


def gla(proj, lb_params, b_gain, s0, li):
    bn, t_len, ab_in = proj.shape
    n_ab = lb_params.shape[0]
    cin = min(B_CHUNK, t_len)
    nc = t_len // cin
    base = ab_in // B_DIM - 4 * B_HEADS

    hp = B_HEADS_PER_STEP
    wid = hp * B_DIM
    base = base // hp

    def col(j):
        return pl.BlockSpec((None, cin, wid), lambda b, h, c, j=j: (b, c, base + j * (B_HEADS // hp) + h))

    out_dtype = F32 if cin < 16 else BF16
    return pl.pallas_call(
        functools.partial(_gla_kernel, li=li, cin=cin),
        grid=(bn, B_HEADS // hp, nc),
        in_specs=[col(0), col(1), col(2), col(3),
                  pl.BlockSpec((n_ab, wid), lambda b, h, c: (0, h)),
                  pl.BlockSpec((None, 1, wid), lambda b, h, c: (li, 0, h)),
                  pl.BlockSpec((None, hp, B_DIM, B_DIM), lambda b, h, c: (b, h, 0, 0))],
        out_specs=[pl.BlockSpec((None, cin, wid), lambda b, h, c: (b, c, h)),
                   pl.BlockSpec((None, hp, B_DIM, B_DIM), lambda b, h, c: (b, h, 0, 0))],
        out_shape=[jax.ShapeDtypeStruct((bn, t_len, B_HEADS * B_DIM), out_dtype),
                   jax.ShapeDtypeStruct((bn, B_HEADS, B_DIM, B_DIM), F32)],
        scratch_shapes=[pltpu.VMEM((hp, B_DIM, B_DIM), F32), pltpu.VMEM((hp, B_CHUNK, B_CHUNK), F32),
                        pltpu.VMEM((hp, 3, B_CHUNK, B_DIM), F32)],
        compiler_params=_cparams(("parallel", "parallel", "arbitrary"), 32),
        name="gla",
    )(proj, proj, proj, proj, lb_params, b_gain.reshape(n_ab, 1, -1), s0)


def _mix_kernel(x_ref, xp_ref, sh_ref, g_ref, mu_ref, *outs):
    xm_refs, hl_ref = outs[:6], outs[6]
    i = pl.program_id(1)
    tt = x_ref.shape[0]
    g = g_ref[...]
    h = _rms(x_ref[...], g)
    hp = _rms(xp_ref[...], g)
    first = jnp.where(i == 0, sh_ref[...], hp[7:8])
    rolled = pltpu.roll(h, 1, axis=0)
    row = lax.broadcasted_iota(jnp.int32, h.shape, 0)
    prev = jnp.where(row == 0, first, rolled)
    xx = prev - h
    for j in range(6):
        xm_refs[j][...] = (h + xx * mu_ref[j:j + 1]).astype(BF16)

    @pl.when(i == pl.num_programs(1) - 1)
    def _():
        hl_ref[...] = h[tt - 1:tt]


def rwkv_mix(x, shift, g, mu):
    bn, t_len, d = x.shape
    tt = min(t_len, 256)
    sub = tt // 8
    tile = pl.BlockSpec((None, tt, d), lambda b, i: (b, i, 0))
    *xm, hl = pl.pallas_call(
        _mix_kernel,
        grid=(bn, t_len // tt),
        in_specs=[pl.BlockSpec((None, tt, d), lambda b, i: (b, i, 0)),
                  pl.BlockSpec((None, 8, d), lambda b, i: (b, jnp.maximum(i * sub - 1, 0), 0)),
                  pl.BlockSpec((None, 1, d), lambda b, i: (b, 0, 0)),
                  pl.BlockSpec((1, d), lambda b, i: (0, 0)),
                  pl.BlockSpec((6, d), lambda b, i: (0, 0))],
        out_specs=[tile] * 6 + [pl.BlockSpec((None, 1, d), lambda b, i: (b, 0, 0))],
        out_shape=[jax.ShapeDtypeStruct((bn, t_len, d), BF16)] * 6 + [jax.ShapeDtypeStruct((bn, 1, d), F32)],
        compiler_params=_cparams(("parallel", "arbitrary"), 40),
        name="rwkv_mix",
    )(x, x, shift.reshape(bn, 1, d), g.reshape(1, d), mu)
    return [t.reshape(bn * t_len, d) for t in xm], hl.reshape(bn, d)


def _scan_kernel(r_ref, k_ref, v_ref, w_ref, a_ref, par_ref, s0_ref,
                 o_ref, sfin_ref, st, ops, *, tc):
    c = pl.program_id(1)
    n = C_HEAD_DIM

    @pl.when(c == 0)
    def _():
        st[...] = s0_ref[...]

    def key_row(j, ki):
        return ops[j, pl.ds(ki, 1), :]

    def step(t, carry):
        c_kk, c_ka, c_rk, ln_w, ln_b = (par_ref[j] for j in range(5))
        w = w_ref[t]
        a = a_ref[t]
        kt = k_ref[t]
        kk = kt * c_kk
        nrm = jnp.sqrt(jnp.sum(kk * kk, axis=0, keepdims=True))
        kk = kk / jnp.maximum(nrm, 1e-12)
        k2 = kt * (1.0 + (a - 1.0) * c_ka)
        rt = r_ref[t]
        vt = v_ref[t]
        ops[0] = -kk
        ops[1] = w
        ops[2] = kk * a
        ops[3] = k2
        ops[4] = rt
        sa0 = st[0] * key_row(0, 0)
        sa1 = st[1] * key_row(0, 1)
        for ki in range(2, n, 2):
            sa0 = sa0 + st[ki] * key_row(0, ki)
            sa1 = sa1 + st[ki + 1] * key_row(0, ki + 1)
        sa = sa0 + sa1
        o0 = o1 = None
        for ki in range(n):
            s_n = st[ki] * key_row(1, ki) + sa * key_row(2, ki) + vt * key_row(3, ki)
            st[ki] = s_n
            term = s_n * key_row(4, ki)
            if ki % 2 == 0:
                o0 = term if o0 is None else o0 + term
            else:
                o1 = term if o1 is None else o1 + term
        o = o0 + o1
        mean = jnp.mean(o, axis=0, keepdims=True)
        var = jnp.mean(jnp.square(o - mean), axis=0, keepdims=True)
        o = (o - mean) * lax.rsqrt(var + C_GN_EPS) * ln_w + ln_b
        bonus = jnp.sum(rt * k2 * c_rk, axis=0, keepdims=True)
        o_ref[t] = o + bonus * vt
        return carry

    lax.fori_loop(0, tc, step, 0)

    @pl.when(c == pl.num_programs(1) - 1)
    def _():
        sfin_ref[...] = st[...]


def rwkv_scan(r, k, v, decay, iclr, par, s0):
    t_len, n, bh = r.shape
    tc = min(t_len, 32)
    seq = pl.BlockSpec((tc, n, LANES), lambda l, c: (c, 0, l))
    return pl.pallas_call(
        functools.partial(_scan_kernel, tc=tc),
        grid=(bh // LANES, t_len // tc),
        in_specs=[seq] * 5 + [pl.BlockSpec((5, n, LANES), lambda l, c: (0, 0, l)),
                              pl.BlockSpec((n, n, LANES), lambda l, c: (0, 0, l))],
        out_specs=[seq, pl.BlockSpec((n, n, LANES), lambda l, c: (0, 0, l))],
        out_shape=[jax.ShapeDtypeStruct((t_len, n, bh), F32),
                   jax.ShapeDtypeStruct((n, n, bh), F32)],
        scratch_shapes=[pltpu.VMEM((n, n, LANES), F32), pltpu.VMEM((5, n, LANES), F32)],
        compiler_params=_cparams(("parallel", "arbitrary"), 40),
        name="rwkv_scan",
    )(r, k, v, decay, iclr, par, s0)


def _tiles_time(t_len):
    return t_len % 256 == 0


def _pad_to(x, axis, mult):
    pad = (-x.shape[axis]) % mult
    if pad == 0:
        return x
    widths = [(0, 0)] * x.ndim
    widths[axis] = (0, pad)
    return jnp.pad(x, widths)


def _ab_layer(x, h, li, p, caches, b_state, prev_bufs):
    bn, t_len, d = x.shape
    n_ab = p["w_in_ab"].shape[0]
    proj = matmul(h, p["w_in_ab"], (li,)).reshape(bn, t_len, -1)
    if caches is None:
        a_out = attn_prompt(proj)
        new_bufs = [kv_pack(proj, g, li, n_ab, None if prev_bufs is None else prev_bufs[g]) for g in range(3)]
        s0 = jnp.zeros((bn, B_HEADS, B_DIM, B_DIM), F32)
    else:
        a_out, new_bufs = attn_sample(proj, caches, li, prev_bufs)
        a_out = a_out.astype(BF16)
        s0 = b_state[li]
    b_out, s_b = gla(proj, p["b_lower_bounds"], p["b_norm_gain"], s0, li)
    mix_in = jnp.concatenate([a_out, b_out.astype(BF16)], axis=-1).reshape(bn * t_len, -1)
    return mix_in, new_bufs, s_b


def _rwkv_layer(x, li, g_pre, p, shift, wkv):
    bn, t_len, d = x.shape
    n = C_HEAD_DIM
    hh = d // n
    (xr, xw, xk, xv, xa, xg), h_last = rwkv_mix(x, shift, g_pre, p["c_mu"][li])

    def proj_tm(a, w_t, widx, bias=None, act=None):
        if _tiles_time(t_len):
            return matmul(a, w_t, widx, bias=bias, act=act, tb=(bn, t_len), w_t=True)
        y = matmul(a, w_t, widx, bias=bias, act=act, w_t=True)
        return jnp.swapaxes(y.reshape(bn, t_len, -1), 1, 2).reshape(-1, t_len)

    r = proj_tm(xr, p["c_w_rkv_t"], (li, 0))
    k = proj_tm(xk, p["c_w_rkv_t"], (li, 1))
    v = proj_tm(xv, p["c_w_rkv_t"], (li, 2))
    w_lo = matmul(xw, p["c_w1"], (li,), act="tanh", out_dtype=BF16)
    decay = proj_tm(w_lo, p["c_w2_t"], (li,), bias=p["c_w0"][li], act="rwkv_decay")
    a_lo = matmul(xa, p["c_a1"], (li,), out_dtype=BF16)
    iclr = proj_tm(a_lo, p["c_a2_t"], (li,), bias=p["c_a0"][li], act="sigmoid")
    g_lo = matmul(xg, p["c_g1"], (li,), act="sigmoid", out_dtype=BF16)
    gate = proj_tm(g_lo, p["c_g2_t"], (li,))

    def to_scan(t):
        return jnp.transpose(t.reshape(bn * hh, n, t_len), (2, 1, 0))

    def lanes(vec):
        return jnp.tile(vec.reshape(hh, n).T, (1, bn))

    par = jnp.stack([lanes(p["c_k_k"][li]), lanes(p["c_k_a"][li]), lanes(p["c_r_k"][li].reshape(-1)),
                     lanes(p["c_ln_w"][li]), lanes(p["c_ln_b"][li])])
    s0 = jnp.transpose(wkv, (3, 2, 0, 1)).reshape(n, n, bn * hh)
    o, s_fin = rwkv_scan(*(to_scan(t) for t in (r, k, v, decay, iclr)), par, s0)
    o = jnp.transpose(o, (2, 1, 0)).reshape(bn * d, t_len)
    s_fin = jnp.transpose(s_fin.reshape(n, n, bn, hh), (2, 3, 1, 0))
    return o, gate, s_fin, h_last


def _prep_params(raw):
    to16 = lambda w: w.astype(BF16)

    def to16_t(w):
        return jnp.swapaxes(_pad_to(w, w.ndim - 2, LANES), -1, -2).astype(BF16)

    p = dict(raw)
    for name in ("w_out_ab", "c_w_out", "w_ffn_up", "w_ffn_down"):
        p[name] = to16(raw[name])
    for name in ("c_w_rkv", "c_w2", "c_a2", "c_g2"):
        p[name + "_t"] = to16_t(raw[name])
        del p[name]
    for name in ("c_w1", "c_a1", "c_g1"):
        p[name] = _pad_to(raw[name], 2, LANES)
    return p


def _trunk(x, caches, b_states, c_wkv, c_shift, p):
    bn, t_len, d = x.shape
    m = bn * t_len
    depth = p["norm_gains"].shape[0]
    kv_bufs = None
    if caches is not None:
        caches = [c.reshape(c.shape[0], c.shape[1], -1, c.shape[-1]) for c in caches]
    new_b, new_wkv, new_shift = [], [], []
    h = rmsnorm_rows(x.reshape(m, d), p["norm_gains"][0, 0])
    for layer in range(depth):
        gains = p["norm_gains"][layer]
        li = layer // 2
        last = layer == depth - 1
        next_pre = p["norm_gains"][layer + 1, 0] if not last else gains[0]
        if layer % 2 == 0:
            mix_in, kv_bufs, s_b = _ab_layer(x, h, li, p, caches, b_states, kv_bufs)
            new_b.append(s_b)
            x2, h2 = matmul_post(mix_in, p["w_out_ab"], li, gains[1], x.reshape(m, d), gains[2])
        else:
            wkv0 = jnp.zeros((bn, d // C_HEAD_DIM, C_HEAD_DIM, C_HEAD_DIM), F32) if c_wkv is None else c_wkv[li]
            sh0 = jnp.zeros((bn, d), F32) if c_shift is None else c_shift[li]
            o, gate, s_wkv, s_shift = _rwkv_layer(x, li, gains[0], p, sh0, wkv0)
            new_wkv.append(s_wkv)
            new_shift.append(s_shift)
            if _tiles_time(t_len):
                tb = (bn, t_len)
            else:
                tb = None
                o, gate = (jnp.swapaxes(t.reshape(bn, d, t_len), 1, 2).reshape(m, d) for t in (o, gate))
            x2, h2 = matmul_post(o, p["c_w_out"], li, gains[1], x.reshape(m, d), gains[2], gate=gate, tb=tb)
        next_even = (layer + 1) % 2 == 0 and not last
        x3, h = ffn(h2, p["w_ffn_up"], p["w_ffn_down"], layer, gains[3], x2, next_pre, emit_h=next_even)
        x = x3.reshape(bn, t_len, d)
    a_states = tuple(kv.reshape(kv.shape[0], bn, -1, 2, A_HEADS, A_HEAD_DIM) for kv in kv_bufs)
    return x, a_states, jnp.stack(new_b), jnp.stack(new_wkv), jnp.stack(new_shift)


def kernel(x_prompt, x_sample, cache_a1_kv, cache_a2_kv, cache_a3_kv, state_b, state_c_wkv, state_c_shift, norm_gains, w_in_ab, w_out_ab, b_lower_bounds, b_norm_gain, c_mu, c_w_rkv, c_w0, c_w1, c_w2, c_a0, c_a1, c_a2, c_g1, c_g2, c_k_k, c_k_a, c_r_k, c_ln_w, c_ln_b, c_w_out, w_ffn_up, w_ffn_down):
    p = _prep_params(dict(
        norm_gains=norm_gains, w_in_ab=w_in_ab, w_out_ab=w_out_ab, b_lower_bounds=b_lower_bounds,
        b_norm_gain=b_norm_gain, c_mu=c_mu, c_w_rkv=c_w_rkv, c_w0=c_w0, c_w1=c_w1, c_w2=c_w2, c_a0=c_a0,
        c_a1=c_a1, c_a2=c_a2, c_g1=c_g1, c_g2=c_g2, c_k_k=c_k_k, c_k_a=c_k_a, c_r_k=c_r_k, c_ln_w=c_ln_w,
        c_ln_b=c_ln_b, c_w_out=c_w_out, w_ffn_up=w_ffn_up, w_ffn_down=w_ffn_down))
    y_p, (pa1, pa2, pa3), pb, pwkv, pshift = _trunk(x_prompt, None, None, None, None, p)
    y_s, (sa1, sa2, sa3), sb, swkv, sshift = _trunk(
        x_sample, (cache_a1_kv, cache_a2_kv, cache_a3_kv), state_b, state_c_wkv, state_c_shift, p)
    return (y_p, y_s, pa1, pa2, pa3, pb, pwkv, pshift, sa1, sa2, sa3, sb, swkv, sshift)
```

```python
import functools

import jax
import jax.numpy as jnp
from jax import lax
from jax.experimental import pallas as pl
from jax.experimental.pallas import tpu as pltpu

F32 = jnp.float32
BF16 = jnp.bfloat16

NORM_EPS = 1e-6
A_GROUPS = ((128, 1), (512, 4), (2048, 16))
A_HEADS = 8
A_HEAD_DIM = 128
A_WIDTH = A_HEADS * A_HEAD_DIM
A_BLK = 128
B_HEADS = 8
B_DIM = 128
B_CHUNK = 128
B_SUB = 8
B_HEADS_PER_STEP = 2
C_HEAD_DIM = 64
C_GN_EPS = 64e-5
LANES = 128
MIB = 1024 * 1024


def _cparams(sem, vmem_mib):
    return pltpu.CompilerParams(dimension_semantics=sem, vmem_limit_bytes=vmem_mib * MIB)


def _rms(x, g):
    return x * lax.rsqrt(jnp.mean(x * x, axis=-1, keepdims=True) + NORM_EPS) * g


def _dot(a, b):
    return jnp.dot(a, b, preferred_element_type=F32)


def _dot_nt(a, b):
    return lax.dot_general(a, b, (((1,), (1,)), ((), ())), preferred_element_type=F32)


def _dot_tn(a, b):
    return lax.dot_general(a, b, (((0,), (0,)), ((), ())), preferred_element_type=F32)


def _log2(n):
    assert n > 0 and n & (n - 1) == 0, n
    return n.bit_length() - 1


def _sigmoid(x):
    return 1.0 / (1.0 + jnp.exp(-x))


def _silu(x):
    return x * _sigmoid(x)


def _norm_kernel(x_ref, g_ref, o_ref):
    o_ref[...] = _rms(x_ref[...], g_ref[...]).astype(o_ref.dtype)


def rmsnorm_rows(x, g):
    m, d = x.shape
    tm = min(m, 512)
    return pl.pallas_call(
        _norm_kernel,
        grid=(m // tm,),
        in_specs=[pl.BlockSpec((tm, d), lambda i: (i, 0)),
                  pl.BlockSpec((1, d), lambda i: (0, 0))],
        out_specs=pl.BlockSpec((tm, d), lambda i: (i, 0)),
        out_shape=jax.ShapeDtypeStruct((m, d), BF16),
        compiler_params=_cparams(("parallel",), 32),
        name="rmsnorm_rows",
    )(x, g.reshape(1, d))


def _rwkv_decay(z):
    y = -z
    softplus = jnp.maximum(y, 0.0) + jnp.log(1.0 + jnp.exp(-jnp.abs(y)))
    return jnp.exp(-jnp.exp(-softplus - 0.5))


def _mm_kernel(*refs, act, has_bias, w_t, out_t):
    if has_bias:
        a_ref, w_ref, b_ref, o_ref = refs
    else:
        a_ref, w_ref, o_ref = refs
    w = w_ref[...].astype(BF16)
    if out_t:
        y = _dot_nt(w, a_ref[...])
    elif w_t:
        y = _dot_nt(a_ref[...], w)
    else:
        y = _dot(a_ref[...], w)
    if has_bias:
        y = y + b_ref[...]
    if act == "tanh":
        y = jnp.tanh(y)
    elif act == "sigmoid":
        y = _sigmoid(y)
    elif act == "rwkv_decay":
        y = _rwkv_decay(y)
    o_ref[...] = y.astype(o_ref.dtype)


MM_BLOCK_BUDGET = 40 * MIB
MM_VMEM_LIMIT_MIB = 52


def _mm_tiles(rows, k, n, w_itemsize, out_itemsize):
    def fits(tm, tn):
        return 2 * (tm * k * 2 + k * tn * w_itemsize + tm * tn * out_itemsize) <= MM_BLOCK_BUDGET

    tms = [t for t in (2048, 1024, 512, 256, 128, 64, 32, 16) if t <= rows and rows % t == 0] or [rows]
    tns = [t for t in (2048, 1024, 512, 256, 128) if t <= n and n % t == 0] or [n]
    if n <= tns[0] and 2 * k * n * w_itemsize <= MM_BLOCK_BUDGET // 2:
        for tm in tms:
            if fits(tm, n):
                return tm, n
    for tm in tms:
        for tn in tns:
            if fits(tm, tn):
                return tm, tn
    return tms[-1], tns[-1]


def matmul(a, w, widx=(), bias=None, act=None, out_dtype=F32, tb=None, w_t=False):
    m, k = a.shape
    n = w.shape[-2] if w_t else w.shape[-1]
    assert (w.shape[-1] if w_t else w.shape[-2]) == k and (tb is None or w_t)
    tm, tn = _mm_tiles(m if tb is None else tb[1], k, n, w.dtype.itemsize, jnp.dtype(out_dtype).itemsize)
    nj = n // tn
    if tb is None:
        out_shape = (m, n)
        out_block = (tm, tn)
        out_index = lambda i, j: (i, j)
    else:
        bn, t_len = tb
        nt = t_len // tm
        out_shape = (bn * n, t_len)
        out_block = (tn, tm)
        out_index = lambda i, j: ((i // nt) * nj + j, i % nt)
    lead = (None,) * len(widx)
    in_specs = [pl.BlockSpec((tm, k), lambda i, j: (i, 0))]
    if w_t:
        in_specs.append(pl.BlockSpec(lead + (tn, k), lambda i, j: widx + (j, 0)))
    else:
        in_specs.append(pl.BlockSpec(lead + (k, tn), lambda i, j: widx + (0, j)))
    args = [a, w]
    if bias is not None:
        if tb is None:
            in_specs.append(pl.BlockSpec((1, tn), lambda i, j: (0, j)))
            args.append(bias.reshape(1, n))
        else:
            in_specs.append(pl.BlockSpec((tn, 1), lambda i, j: (j, 0)))
            args.append(bias.reshape(n, 1))
    return pl.pallas_call(
        functools.partial(_mm_kernel, act=act, has_bias=bias is not None, w_t=w_t, out_t=tb is not None),
        grid=(m // tm, nj),
        in_specs=in_specs,
        out_specs=pl.BlockSpec(out_block, out_index),
        out_shape=jax.ShapeDtypeStruct(out_shape, out_dtype),
        compiler_params=_cparams(("parallel", "parallel"), MM_VMEM_LIMIT_MIB),
        name="matmul",
    )(*args)


def _mm_post_kernel(*refs, has_gate, emit_h, transpose_in):
    refs = list(refs)
    a_ref = refs.pop(0)
    gate_ref = refs.pop(0) if has_gate else None
    w_ref, g1_ref, xres_ref, gn_ref, x_out = refs[:5]
    a = a_ref[...]
    if has_gate:
        a = a * gate_ref[...]
    a = a.astype(BF16)
    y = _dot_tn(a, w_ref[...]) if transpose_in else _dot(a, w_ref[...])
    x_new = xres_ref[...] + _rms(y, g1_ref[...])
    x_out[...] = x_new
    if emit_h:
        refs[5][...] = _rms(x_new, gn_ref[...]).astype(BF16)


def matmul_post(a, w, li, g_post, xres, g_next, gate=None, emit_h=True, tb=None):
    m, d = xres.shape
    k = w.shape[1]
    tm = min(m, 256) if tb is None else min(tb[1], 256)
    row = lambda i: (i, 0)
    fixed = lambda i: (0, 0)
    if tb is None:
        a_spec = pl.BlockSpec((tm, k), row)
    else:
        nt = tb[1] // tm
        a_spec = pl.BlockSpec((k, tm), lambda i: (i // nt, i % nt))
    in_specs = [a_spec]
    args = [a]
    if gate is not None:
        in_specs.append(a_spec)
        args.append(gate)
    in_specs += [pl.BlockSpec((None, k, d), lambda i: (li, 0, 0)), pl.BlockSpec((1, d), fixed),
                 pl.BlockSpec((tm, d), row), pl.BlockSpec((1, d), fixed)]
    args += [w, g_post.reshape(1, d), xres, g_next.reshape(1, d)]
    out_specs = [pl.BlockSpec((tm, d), row)]
    out_shape = [jax.ShapeDtypeStruct((m, d), F32)]
    if emit_h:
        out_specs.append(pl.BlockSpec((tm, d), row))
        out_shape.append(jax.ShapeDtypeStruct((m, d), BF16))
    res = pl.pallas_call(
        functools.partial(_mm_post_kernel, has_gate=gate is not None, emit_h=emit_h, transpose_in=tb is not None),
        grid=(m // tm,),
        in_specs=in_specs,
        out_specs=out_specs,
        out_shape=out_shape,
        compiler_params=_cparams(("parallel",), 48),
        name="matmul_post",
    )(*args)
    return (res[0], res[1]) if emit_h else (res[0], None)


def _ffn_kernel(h_ref, wg_ref, wu_ref, wd_ref, g1_ref, xres_ref, gn_ref, *outs, emit_h):
    if emit_h:
        x_out, h_out, acc_ref = outs
    else:
        x_out, acc_ref = outs
    j = pl.program_id(1)

    @pl.when(j == 0)
    def _():
        acc_ref[...] = jnp.zeros_like(acc_ref)

    h = h_ref[...]
    gate = _dot(h, wg_ref[...])
    up = _dot(h, wu_ref[...])
    act = (_silu(gate) * up).astype(BF16)
    acc_ref[...] += _dot(act, wd_ref[...])

    @pl.when(j == pl.num_programs(1) - 1)
    def _():
        x_new = xres_ref[...] + _rms(acc_ref[...], g1_ref[...])
        x_out[...] = x_new
        if emit_h:
            h_out[...] = _rms(x_new, gn_ref[...]).astype(BF16)


def ffn(h, w_up, w_down, layer, g_post, xres, g_next, emit_h=True):
    m, d = h.shape
    f = w_down.shape[1]
    tm = min(m, 512)
    th = 512
    nj = f // th
    row = lambda i, j: (i, 0)
    fixed = lambda i, j: (0, 0)
    in_specs = [pl.BlockSpec((tm, d), row),
                pl.BlockSpec((None, d, th), lambda i, j: (layer, 0, j)),
                pl.BlockSpec((None, d, th), lambda i, j: (layer, 0, j + nj)),
                pl.BlockSpec((None, th, d), lambda i, j: (layer, j, 0)),
                pl.BlockSpec((1, d), fixed),
                pl.BlockSpec((tm, d), row),
                pl.BlockSpec((1, d), fixed)]
    out_specs = [pl.BlockSpec((tm, d), row)]
    out_shape = [jax.ShapeDtypeStruct((m, d), F32)]
    if emit_h:
        out_specs.append(pl.BlockSpec((tm, d), row))
        out_shape.append(jax.ShapeDtypeStruct((m, d), BF16))
    res = pl.pallas_call(
        functools.partial(_ffn_kernel, emit_h=emit_h),
        grid=(m // tm, nj),
        in_specs=in_specs,
        out_specs=out_specs,
        out_shape=out_shape,
        scratch_shapes=[pltpu.VMEM((tm, d), F32)],
        compiler_params=_cparams(("parallel", "arbitrary"), 52),
        name="ffn",
    )(h, w_up, w_up, w_down, g_post.reshape(1, d), xres, g_next.reshape(1, d))
    return (res[0], res[1]) if emit_h else (res[0], None)


def _alibi_slope(shape):
    h = pl.program_id(1)
    expo = (h + 1).astype(F32) * (-8.0 / A_HEADS)
    return jnp.exp2(jnp.full(shape, 1.0, F32) * expo)


def _attn_prompt_kernel(q1, k1, v1, q2, k2, v2, q3, k3, v3, out_ref, o_scr, l_scr):
    t_len = out_ref.shape[0]
    blk = A_BLK
    scale = A_HEAD_DIM ** -0.5
    slope = _alibi_slope((blk, blk))
    qi = lax.broadcasted_iota(jnp.int32, (blk, blk), 0)
    ki = lax.broadcasted_iota(jnp.int32, (blk, blk), 1)
    dist_cur = (qi - ki).astype(F32)
    dist_prev = (qi - ki + blk).astype(F32)
    mask_cur = ki <= qi
    mask_prev = ki >= qi
    neg_inf = jnp.float32(-jnp.inf)

    def rows(start, dil):
        if dil == 1:
            return pl.ds(pl.multiple_of(start, blk), blk)
        return pl.ds(start, blk, stride=dil)

    def one_block(gi, q_ref, k_ref, v_ref, dil, start, prev_start, use_prev):
        q = q_ref[rows(start, dil), :].astype(BF16)
        kc = k_ref[rows(start, dil), :].astype(BF16)
        vc = v_ref[rows(start, dil), :].astype(BF16)
        s_c = _dot_nt(q, kc) * scale - slope * (dist_cur * float(dil))
        s_c = jnp.where(mask_cur, s_c, neg_inf)
        mx = jnp.max(s_c, axis=-1, keepdims=True)
        if use_prev is not None:
            kp = k_ref[rows(prev_start, dil), :].astype(BF16)
            vp = v_ref[rows(prev_start, dil), :].astype(BF16)
            s_p = _dot_nt(q, kp) * scale - slope * (dist_prev * float(dil))
            s_p = jnp.where(mask_prev, s_p + jnp.where(use_prev, 0.0, neg_inf), neg_inf)
            mx = jnp.maximum(mx, jnp.max(s_p, axis=-1, keepdims=True))
        p_c = jnp.exp(s_c - mx)
        den = jnp.sum(p_c, axis=-1, keepdims=True)
        acc = _dot(p_c.astype(BF16), vc)
        if use_prev is not None:
            p_p = jnp.exp(s_p - mx)
            den = den + jnp.sum(p_p, axis=-1, keepdims=True)
            acc = acc + _dot(p_p.astype(BF16), vp)
        o = acc / den
        lse = mx + jnp.log(den)
        o_scr[gi, rows(start, dil), :] = o
        l_scr[gi, rows(start, dil), :] = jnp.broadcast_to(lse, (blk, LANES))

    for gi, (q_ref, k_ref, v_ref) in enumerate(((q1, k1, v1), (q2, k2, v2), (q3, k3, v3))):
        window, dil = A_GROUPS[gi]
        assert window // dil == blk
        m_len = t_len // dil
        nb = m_len // blk
        assert nb * blk * dil == t_len

        def body(idx, carry, gi=gi, q_ref=q_ref, k_ref=k_ref, v_ref=v_ref, dil=dil, nb=nb):
            r = idx // nb
            n = idx - r * nb
            start = n * (blk * dil) + r
            if nb == 1:
                one_block(gi, q_ref, k_ref, v_ref, dil, start, None, None)
            else:
                prev_start = jnp.maximum(n - 1, 0) * (blk * dil) + r
                one_block(gi, q_ref, k_ref, v_ref, dil, start, prev_start, n > 0)
            return carry

        lax.fori_loop(0, dil * nb, body, 0, unroll=16)

    def merge(c, carry):
        sl = pl.ds(pl.multiple_of(c * blk, blk), blk)
        l0, l1, l2 = l_scr[0, sl, :], l_scr[1, sl, :], l_scr[2, sl, :]
        mx = jnp.maximum(jnp.maximum(l0, l1), l2)
        w0, w1, w2 = jnp.exp(l0 - mx), jnp.exp(l1 - mx), jnp.exp(l2 - mx)
        num = w0 * o_scr[0, sl, :] + w1 * o_scr[1, sl, :] + w2 * o_scr[2, sl, :]
        out_ref[sl, :] = (num / (w0 + w1 + w2)).astype(out_ref.dtype)
        return carry

    lax.fori_loop(0, t_len // blk, merge, 0, unroll=2)


def attn_prompt(proj):
    bn, t_len, _ = proj.shape
    hd = A_HEAD_DIM

    def col(cb):
        return pl.BlockSpec((None, t_len, hd), lambda b, h, cb=cb: (b, 0, cb * A_HEADS + h))

    in_specs = [col(3 * g + j) for g in range(3) for j in range(3)]
    return pl.pallas_call(
        _attn_prompt_kernel,
        grid=(bn, A_HEADS),
        in_specs=in_specs,
        out_specs=pl.BlockSpec((None, t_len, hd), lambda b, h: (b, 0, h)),
        out_shape=jax.ShapeDtypeStruct((bn, t_len, A_WIDTH), BF16),
        scratch_shapes=[pltpu.VMEM((3, t_len, hd), F32), pltpu.VMEM((3, t_len, LANES), F32)],
        compiler_params=_cparams(("parallel", "parallel"), 40),
        name="attn_prompt",
    )(*([proj] * 9))


KV_ROWS = 2 * A_HEADS
SAMPLE_CHUNKS = 4


def _kv_shift_kernel(c_ref, n_ref, o_ref):
    rows = c_ref.shape[0]
    shift = n_ref.shape[0]
    last = pl.program_id(2) == pl.num_programs(2) - 1
    o_ref[0:rows - shift, :] = c_ref[shift:rows, :]
    o_ref[rows - shift:rows, :] = jnp.where(last, jnp.zeros_like(n_ref), n_ref[...])


def kv_shift(cache, l_new):
    n_ab, bn, total, hd = cache.shape
    shift = l_new * KV_ROWS
    rows = total // SAMPLE_CHUNKS
    assert rows % shift == 0 and rows > shift
    per = rows // shift
    n_blk = total // shift
    chunk = pl.BlockSpec((None, None, rows, hd), lambda l, b, c: (l, b, c, 0))
    return pl.pallas_call(
        _kv_shift_kernel,
        grid=(n_ab, bn, SAMPLE_CHUNKS),
        in_specs=[chunk, pl.BlockSpec((None, None, shift, hd),
                                      lambda l, b, c: (l, b, jnp.minimum((c + 1) * per, n_blk - 1), 0))],
        out_specs=chunk,
        out_shape=jax.ShapeDtypeStruct(cache.shape, cache.dtype),
        compiler_params=_cparams(("parallel", "parallel", "parallel"), 32),
        name="kv_shift",
    )(cache, cache)


def _attn_sample_kernel(*refs):
    cols = refs[0:9]
    chunk_refs = refs[9:12]
    out_ref = refs[15]
    tail_refs = refs[16:19]
    m_scr, d_scr, acc_scr, new_scr = refs[19:]
    c = pl.program_id(1)
    n_chunks = pl.num_programs(1)
    last = c == n_chunks - 1
    l_new = out_ref.shape[0]
    shift = l_new * KV_ROWS
    hd = A_HEAD_DIM
    scale = hd ** -0.5
    neg_inf = jnp.float32(-jnp.inf)

    nq = A_HEADS * l_new

    @pl.when(c == 0)
    def _():
        m_scr[...] = jnp.full(m_scr.shape, neg_inf, F32)
        d_scr[...] = jnp.zeros(d_scr.shape, F32)
        acc_scr[...] = jnp.zeros(acc_scr.shape, F32)
        for gi in range(3):
            for kv, src in enumerate(cols[3 * gi + 1:3 * gi + 3]):
                for h in range(A_HEADS):
                    new_scr[gi, pl.ds(kv * A_HEADS + h, l_new, stride=KV_ROWS), :] = src[:, h * hd:(h + 1) * hd]
            tail_refs[gi][...] = new_scr[gi]

    def segment(q_all, buf, pos0, n_buf, window, dil, live):
        rows = buf.shape[0]
        keys_on_value_rows = jnp.concatenate([jnp.zeros((A_HEADS, hd), F32), buf[0:rows - A_HEADS]], axis=0)
        s = _dot_nt(q_all, keys_on_value_rows.astype(BF16))
        qrow = lax.broadcasted_iota(jnp.int32, (nq, rows), 0)
        col = lax.broadcasted_iota(jnp.int32, (nq, rows), 1)
        head = qrow >> _log2(l_new)
        dist = n_buf + (qrow & (l_new - 1)) - (pos0 + (col >> _log2(KV_ROWS)))
        valid = jnp.logical_and(
            jnp.logical_and((col & (KV_ROWS - 1)) == A_HEADS + head, (dist & (dil - 1)) == 0),
            jnp.logical_and(dist >= 0, dist <= window))
        slope = jnp.exp2((head + 1).astype(F32) * (-8.0 / A_HEADS))
        s = jnp.where(valid, s * scale - slope * dist.astype(F32), neg_inf)
        if live is not None:
            s = jnp.where(live, s, neg_inf)
        m_old = m_scr[...]
        m_new = jnp.maximum(m_old, jnp.max(s, axis=-1, keepdims=True))
        alpha = jnp.exp(m_old - m_new)
        p = jnp.exp(s - m_new[:, 0:1])
        d_scr[...] = alpha * d_scr[...] + jnp.sum(p, axis=-1, keepdims=True)
        acc_scr[...] = alpha * acc_scr[...] + _dot(p.astype(BF16), buf.astype(BF16))
        m_scr[...] = m_new

    for gi in range(3):
        window, dil = A_GROUPS[gi]
        q_ref = cols[3 * gi]
        c_ref = chunk_refs[gi]
        rows = c_ref.shape[0]
        p_len = rows // KV_ROWS
        n_buf = p_len * SAMPLE_CHUNKS

        q_all = jnp.concatenate([q_ref[:, h * hd:(h + 1) * hd] for h in range(A_HEADS)], axis=0).astype(BF16)
        segment(q_all, new_scr[gi], n_buf, n_buf, window, dil, c == 0)
        segment(q_all, c_ref[...], c * p_len, n_buf, window, dil, None)

    @pl.when(last)
    def _():
        o = acc_scr[...] / d_scr[...]
        for h in range(A_HEADS):
            out_ref[:, h * hd:(h + 1) * hd] = o[h * l_new:(h + 1) * l_new]


def attn_sample(proj, caches, li, bufs):
    bn, l_new, _ = proj.shape
    hd = A_HEAD_DIM
    shift = l_new * KV_ROWS

    def col(cb):
        return pl.BlockSpec((None, l_new, A_WIDTH), lambda b, c, cb=cb: (b, 0, cb))

    in_specs = [col(3 * g + j) for g in range(3) for j in range(3)]
    args = [proj] * 9
    chunk_specs, out_specs, out_shape = [], [], []
    for g in range(3):
        rows = caches[g].shape[2] // SAMPLE_CHUNKS
        n_blk = caches[g].shape[2] // shift
        chunk_specs.append(pl.BlockSpec((None, None, rows, hd), lambda b, c: (li, b, c, 0)))
        out_specs.append(pl.BlockSpec((None, None, shift, hd), lambda b, c, n_blk=n_blk: (li, b, n_blk - 1, 0)))
        out_shape.append(jax.ShapeDtypeStruct(caches[g].shape, F32))
    in_specs += chunk_specs + [pl.BlockSpec(memory_space=pl.ANY)] * 3
    args += list(caches) + list(bufs)
    aliases = {12 + g: 1 + g for g in range(3)}
    res = pl.pallas_call(
        _attn_sample_kernel,
        grid=(bn, SAMPLE_CHUNKS),
        in_specs=in_specs,
        out_specs=[pl.BlockSpec((None, l_new, A_WIDTH), lambda b, c: (b, 0, 0))] + out_specs,
        out_shape=[jax.ShapeDtypeStruct((bn, l_new, A_WIDTH), F32)] + out_shape,
        scratch_shapes=[pltpu.VMEM((A_HEADS * l_new, LANES), F32)] * 3 + [pltpu.VMEM((3, shift, hd), F32)],
        input_output_aliases=aliases,
        compiler_params=_cparams(("parallel", "arbitrary"), 48),
        name="attn_sample",
    )(*args)
    return res[0], list(res[1:])


def _kv_pack_kernel(*refs, li):
    k_ref, v_ref, o_ref = refs[0], refs[1], refs[-1]
    tm = k_ref.shape[0]
    hd = A_HEAD_DIM
    slab = pl.program_id(0)

    @pl.when(slab == li)
    def _():
        for kv, src in enumerate((k_ref, v_ref)):
            for h in range(A_HEADS):
                o_ref[pl.ds(kv * A_HEADS + h, tm, stride=KV_ROWS), :] = src[:, h * hd:(h + 1) * hd]

    @pl.when(slab != li)
    def _():
        o_ref[...] = jnp.zeros_like(o_ref)


def kv_pack(proj, g, li, n_ab, prev):
    bn, t_len, _ = proj.shape
    hd = A_HEAD_DIM
    w = min(A_GROUPS[g][0], t_len)
    tm = min(w, 128)
    row0 = (t_len - w) // tm
    in_specs = [pl.BlockSpec((None, tm, A_WIDTH), lambda s, b, i: (b, row0 + i, 3 * g + 1)),
                pl.BlockSpec((None, tm, A_WIDTH), lambda s, b, i: (b, row0 + i, 3 * g + 2))]
    args = [proj, proj]
    aliases = {}
    if prev is None:
        slabs, first = n_ab, 0
    else:
        slabs, first = 1, li
        aliases[2] = 0
        in_specs.append(pl.BlockSpec(memory_space=pl.ANY))
        args.append(prev)
    return pl.pallas_call(
        functools.partial(_kv_pack_kernel, li=li - first),
        grid=(slabs, bn, w // tm),
        in_specs=in_specs,
        out_specs=pl.BlockSpec((None, None, tm * KV_ROWS, hd), lambda s, b, i: (first + s, b, i, 0)),
        out_shape=jax.ShapeDtypeStruct((n_ab, bn, w * KV_ROWS, hd), F32),
        input_output_aliases=aliases,
        compiler_params=_cparams(("parallel", "parallel", "parallel"), 32),
        name="kv_pack",
    )(*args)


def _gla_kernel(bq_ref, bf_ref, bi_ref, bg_ref, lbp_ref, gain_ref, s0_ref,
                o_ref, sfin_ref, st_scr, att_scr, qkb_scr, *, li, cin):
    c = pl.program_id(2)
    ch = B_CHUNK
    last = c == pl.num_programs(2) - 1

    def pad(x):
        if cin == ch:
            return x
        return jnp.concatenate([x, jnp.zeros((ch - cin, x.shape[1]), x.dtype)], axis=0)

    row = lax.broadcasted_iota(jnp.int32, (ch, B_DIM), 0)
    lane = lax.broadcasted_iota(jnp.int32, (B_SUB, ch), 1)
    row_sub = lax.broadcasted_iota(jnp.int32, (B_SUB, B_DIM), 0)
    live = row < cin
    tri = (row >= lax.broadcasted_iota(jnp.int32, (ch, ch), 1)).astype(F32)

    @pl.when(c == 0)
    def _():
        for hj in range(B_HEADS_PER_STEP):
            st_scr[hj] = s0_ref[hj].T

    for hj in range(B_HEADS_PER_STEP):
        hs = slice(hj * B_DIM, (hj + 1) * B_DIM)
        lbp = lbp_ref[:, hs]
        e = jnp.exp(lbp - jnp.max(lbp, axis=0, keepdims=True))
        soft = e / jnp.sum(e, axis=0, keepdims=True)
        lb = jnp.sum(soft[0:li + 1], axis=0, keepdims=True) - soft[0:1]

        q = _silu(pad(bq_ref[:, hs]))
        fgate = lb + (1.0 - lb) * _sigmoid(pad(bf_ref[:, hs]))
        g = jnp.where(live, jnp.log(fgate), 0.0)
        kk = jnp.where(live, 1.0 - fgate, 0.0)
        v16 = pad(bi_ref[:, hs]).astype(BF16)

        b = jnp.dot(tri, g, preferred_element_type=F32, precision=lax.Precision.HIGHEST)
        b_last = b[ch - 1:ch]
        st = st_scr[hj]

        o = _dot_nt((q * jnp.exp(b)).astype(BF16), st.astype(BF16))

        q_scr, kk_scr, b_scr = (qkb_scr.at[hj, j] for j in range(3))
        q_scr[...] = q
        kk_scr[...] = kk
        b_scr[...] = b

        for qb in range(ch // B_SUB):
            r0 = qb * B_SUB
            q_s = q_scr[r0:r0 + B_SUB, :]
            b_s = b_scr[r0:r0 + B_SUB, :]
            kk_s = kk_scr[r0:r0 + B_SUB, :]
            if qb > 0:
                b_ref = b_scr[r0 - 1:r0, :]
                qf = q_s * jnp.exp(b_s - b_ref)
                earlier = row < r0
                kf = jnp.where(earlier, kk_scr[...] * jnp.exp(jnp.where(earlier, b_ref - b_scr[...], 0.0)), 0.0)
                att = _dot_nt(qf.astype(BF16), kf.astype(BF16))
            else:
                att = jnp.zeros((B_SUB, ch), F32)
            for s in range(B_SUB):
                causal = row_sub >= s
                dec = jnp.where(causal, jnp.exp(jnp.where(causal, b_s - b_s[s:s + 1], 0.0)), 0.0)
                colv = jnp.sum(q_s * dec * kk_s[s:s + 1], axis=-1, keepdims=True)
                att = jnp.where(lane == r0 + s, colv, att)
            att_scr[hj, r0:r0 + B_SUB, :] = att

        o = o + _dot(att_scr[hj].astype(BF16), v16)

        kd = kk * jnp.exp(b_last - b)
        st_new = st * jnp.exp(b_last) + _dot_tn(v16, kd.astype(BF16))
        st_scr[hj] = st_new

        on = _rms(o, gain_ref[:, hs]) * _silu(pad(bg_ref[:, hs]))
        o_ref[:, hs] = on[0:cin].astype(o_ref.dtype)

    @pl.when(last)
    def _():
        for hj in range(B_HEADS_PER_STEP):
            sfin_ref[hj] = st_scr[hj].T


def gla(proj, lb_params, b_gain, s0, li):
    bn, t_len, ab_in = proj.shape
    n_ab = lb_params.shape[0]
    cin = min(B_CHUNK, t_len)
    nc = t_len // cin
    base = ab_in // B_DIM - 4 * B_HEADS

    hp = B_HEADS_PER_STEP
    wid = hp * B_DIM
    base = base // hp

    def col(j):
        return pl.BlockSpec((None, cin, wid), lambda b, h, c, j=j: (b, c, base + j * (B_HEADS // hp) + h))

    out_dtype = F32 if cin < 16 else BF16
    return pl.pallas_call(
        functools.partial(_gla_kernel, li=li, cin=cin),
        grid=(bn, B_HEADS // hp, nc),
        in_specs=[col(0), col(1), col(2), col(3),
                  pl.BlockSpec((n_ab, wid), lambda b, h, c: (0, h)),
                  pl.BlockSpec((None, 1, wid), lambda b, h, c: (li, 0, h)),
                  pl.BlockSpec((None, hp, B_DIM, B_DIM), lambda b, h, c: (b, h, 0, 0))],
        out_specs=[pl.BlockSpec((None, cin, wid), lambda b, h, c: (b, c, h)),
                   pl.BlockSpec((None, hp, B_DIM, B_DIM), lambda b, h, c: (b, h, 0, 0))],
        out_shape=[jax.ShapeDtypeStruct((bn, t_len, B_HEADS * B_DIM), out_dtype),
                   jax.ShapeDtypeStruct((bn, B_HEADS, B_DIM, B_DIM), F32)],
        scratch_shapes=[pltpu.VMEM((hp, B_DIM, B_DIM), F32), pltpu.VMEM((hp, B_CHUNK, B_CHUNK), F32),
                        pltpu.VMEM((hp, 3, B_CHUNK, B_DIM), F32)],
        compiler_params=_cparams(("parallel", "parallel", "arbitrary"), 32),
        name="gla",
    )(proj, proj, proj, proj, lb_params, b_gain.reshape(n_ab, 1, -1), s0)


def _mix_kernel(x_ref, xp_ref, sh_ref, g_ref, mu_ref, *outs):
    xm_refs, hl_ref = outs[:6], outs[6]
    i = pl.program_id(1)
    tt = x_ref.shape[0]
    g = g_ref[...]
    h = _rms(x_ref[...], g)
    hp = _rms(xp_ref[...], g)
    first = jnp.where(i == 0, sh_ref[...], hp[7:8])
    rolled = pltpu.roll(h, 1, axis=0)
    row = lax.broadcasted_iota(jnp.int32, h.shape, 0)
    prev = jnp.where(row == 0, first, rolled)
    xx = prev - h
    for j in range(6):
        xm_refs[j][...] = (h + xx * mu_ref[j:j + 1]).astype(BF16)

    @pl.when(i == pl.num_programs(1) - 1)
    def _():
        hl_ref[...] = h[tt - 1:tt]


def rwkv_mix(x, shift, g, mu):
    bn, t_len, d = x.shape
    tt = min(t_len, 256)
    sub = tt // 8
    tile = pl.BlockSpec((None, tt, d), lambda b, i: (b, i, 0))
    *xm, hl = pl.pallas_call(
        _mix_kernel,
        grid=(bn, t_len // tt),
        in_specs=[pl.BlockSpec((None, tt, d), lambda b, i: (b, i, 0)),
                  pl.BlockSpec((None, 8, d), lambda b, i: (b, jnp.maximum(i * sub - 1, 0), 0)),
                  pl.BlockSpec((None, 1, d), lambda b, i: (b, 0, 0)),
                  pl.BlockSpec((1, d), lambda b, i: (0, 0)),
                  pl.BlockSpec((6, d), lambda b, i: (0, 0))],
        out_specs=[tile] * 6 + [pl.BlockSpec((None, 1, d), lambda b, i: (b, 0, 0))],
        out_shape=[jax.ShapeDtypeStruct((bn, t_len, d), BF16)] * 6 + [jax.ShapeDtypeStruct((bn, 1, d), F32)],
        compiler_params=_cparams(("parallel", "arbitrary"), 40),
        name="rwkv_mix",
    )(x, x, shift.reshape(bn, 1, d), g.reshape(1, d), mu)
    return [t.reshape(bn * t_len, d) for t in xm], hl.reshape(bn, d)


def _scan_kernel(r_ref, k_ref, v_ref, w_ref, a_ref, par_ref, s0_ref,
                 o_ref, sfin_ref, st, ops, *, tc):
    c = pl.program_id(1)
    n = C_HEAD_DIM

    @pl.when(c == 0)
    def _():
        st[...] = s0_ref[...]

    def key_row(j, ki):
        return ops[j, pl.ds(ki, 1), :]

    def step(t, carry):
        c_kk, c_ka, c_rk, ln_w, ln_b = (par_ref[j] for j in range(5))
        w = w_ref[t]
        a = a_ref[t]
        kt = k_ref[t]
        kk = kt * c_kk
        nrm = jnp.sqrt(jnp.sum(kk * kk, axis=0, keepdims=True))
        kk = kk / jnp.maximum(nrm, 1e-12)
        k2 = kt * (1.0 + (a - 1.0) * c_ka)
        rt = r_ref[t]
        vt = v_ref[t]
        ops[0] = -kk
        ops[1] = w
        ops[2] = kk * a
        ops[3] = k2
        ops[4] = rt
        sa0 = st[0] * key_row(0, 0)
        sa1 = st[1] * key_row(0, 1)
        for ki in range(2, n, 2):
            sa0 = sa0 + st[ki] * key_row(0, ki)
            sa1 = sa1 + st[ki + 1] * key_row(0, ki + 1)
        sa = sa0 + sa1
        o0 = o1 = None
        for ki in range(n):
            s_n = st[ki] * key_row(1, ki) + sa * key_row(2, ki) + vt * key_row(3, ki)
            st[ki] = s_n
            term = s_n * key_row(4, ki)
            if ki % 2 == 0:
                o0 = term if o0 is None else o0 + term
            else:
                o1 = term if o1 is None else o1 + term
        o = o0 + o1
        mean = jnp.mean(o, axis=0, keepdims=True)
        var = jnp.mean(jnp.square(o - mean), axis=0, keepdims=True)
        o = (o - mean) * lax.rsqrt(var + C_GN_EPS) * ln_w + ln_b
        bonus = jnp.sum(rt * k2 * c_rk, axis=0, keepdims=True)
        o_ref[t] = o + bonus * vt
        return carry

    lax.fori_loop(0, tc, step, 0)

    @pl.when(c == pl.num_programs(1) - 1)
    def _():
        sfin_ref[...] = st[...]


def rwkv_scan(r, k, v, decay, iclr, par, s0):
    t_len, n, bh = r.shape
    tc = min(t_len, 32)
    seq = pl.BlockSpec((tc, n, LANES), lambda l, c: (c, 0, l))
    return pl.pallas_call(
        functools.partial(_scan_kernel, tc=tc),
        grid=(bh // LANES, t_len // tc),
        in_specs=[seq] * 5 + [pl.BlockSpec((5, n, LANES), lambda l, c: (0, 0, l)),
                              pl.BlockSpec((n, n, LANES), lambda l, c: (0, 0, l))],
        out_specs=[seq, pl.BlockSpec((n, n, LANES), lambda l, c: (0, 0, l))],
        out_shape=[jax.ShapeDtypeStruct((t_len, n, bh), F32),
                   jax.ShapeDtypeStruct((n, n, bh), F32)],
        scratch_shapes=[pltpu.VMEM((n, n, LANES), F32), pltpu.VMEM((5, n, LANES), F32)],
        compiler_params=_cparams(("parallel", "arbitrary"), 40),
        name="rwkv_scan",
    )(r, k, v, decay, iclr, par, s0)


def _tiles_time(t_len):
    return t_len % 256 == 0


def _pad_to(x, axis, mult):
    pad = (-x.shape[axis]) % mult
    if pad == 0:
        return x
    widths = [(0, 0)] * x.ndim
    widths[axis] = (0, pad)
    return jnp.pad(x, widths)


def _ab_layer(x, h, li, p, caches, b_state, prev_bufs):
    bn, t_len, d = x.shape
    n_ab = p["w_in_ab"].shape[0]
    proj = matmul(h, p["w_in_ab"], (li,)).reshape(bn, t_len, -1)
    if caches is None:
        a_out = attn_prompt(proj)
        new_bufs = [kv_pack(proj, g, li, n_ab, None if prev_bufs is None else prev_bufs[g]) for g in range(3)]
        s0 = jnp.zeros((bn, B_HEADS, B_DIM, B_DIM), F32)
    else:
        if prev_bufs is None:
            prev_bufs = [kv_shift(c, t_len) for c in caches]
        a_out, new_bufs = attn_sample(proj, caches, li, prev_bufs)
        a_out = a_out.astype(BF16)
        s0 = b_state[li]
    b_out, s_b = gla(proj, p["b_lower_bounds"], p["b_norm_gain"], s0, li)
    mix_in = jnp.concatenate([a_out, b_out.astype(BF16)], axis=-1).reshape(bn * t_len, -1)
    return mix_in, new_bufs, s_b


def _rwkv_layer(x, li, g_pre, p, shift, wkv):
    bn, t_len, d = x.shape
    n = C_HEAD_DIM
    hh = d // n
    (xr, xw, xk, xv, xa, xg), h_last = rwkv_mix(x, shift, g_pre, p["c_mu"][li])

    def proj_tm(a, w_t, widx, bias=None, act=None):
        if _tiles_time(t_len):
            return matmul(a, w_t, widx, bias=bias, act=act, tb=(bn, t_len), w_t=True)
        y = matmul(a, w_t, widx, bias=bias, act=act, w_t=True)
        return jnp.swapaxes(y.reshape(bn, t_len, -1), 1, 2).reshape(-1, t_len)

    r = proj_tm(xr, p["c_w_rkv_t"], (li, 0))
    k = proj_tm(xk, p["c_w_rkv_t"], (li, 1))
    v = proj_tm(xv, p["c_w_rkv_t"], (li, 2))
    w_lo = matmul(xw, p["c_w1"], (li,), act="tanh", out_dtype=BF16)
    decay = proj_tm(w_lo, p["c_w2_t"], (li,), bias=p["c_w0"][li], act="rwkv_decay")
    a_lo = matmul(xa, p["c_a1"], (li,), out_dtype=BF16)
    iclr = proj_tm(a_lo, p["c_a2_t"], (li,), bias=p["c_a0"][li], act="sigmoid")
    g_lo = matmul(xg, p["c_g1"], (li,), act="sigmoid", out_dtype=BF16)
    gate = proj_tm(g_lo, p["c_g2_t"], (li,))

    def to_scan(t):
        return jnp.transpose(t.reshape(bn * hh, n, t_len), (2, 1, 0))

    def lanes(vec):
        return jnp.tile(vec.reshape(hh, n).T, (1, bn))

    par = jnp.stack([lanes(p["c_k_k"][li]), lanes(p["c_k_a"][li]), lanes(p["c_r_k"][li].reshape(-1)),
                     lanes(p["c_ln_w"][li]), lanes(p["c_ln_b"][li])])
    s0 = jnp.transpose(wkv, (3, 2, 0, 1)).reshape(n, n, bn * hh)
    o, s_fin = rwkv_scan(*(to_scan(t) for t in (r, k, v, decay, iclr)), par, s0)
    o = jnp.transpose(o, (2, 1, 0)).reshape(bn * d, t_len)
    s_fin = jnp.transpose(s_fin.reshape(n, n, bn, hh), (2, 3, 1, 0))
    return o, gate, s_fin, h_last


def _prep_params(raw):
    to16 = lambda w: w.astype(BF16)

    def to16_t(w):
        return jnp.swapaxes(_pad_to(w, w.ndim - 2, LANES), -1, -2).astype(BF16)

    p = dict(raw)
    for name in ("w_in_ab", "w_out_ab", "c_w_out", "w_ffn_up", "w_ffn_down"):
        p[name] = to16(raw[name])
    for name in ("c_w_rkv", "c_w2", "c_a2", "c_g2"):
        p[name + "_t"] = to16_t(raw[name])
        del p[name]
    for name in ("c_w1", "c_a1", "c_g1"):
        p[name] = _pad_to(raw[name], 2, LANES)
    return p


def _trunk(x, caches, b_states, c_wkv, c_shift, p):
    bn, t_len, d = x.shape
    m = bn * t_len
    depth = p["norm_gains"].shape[0]
    kv_bufs = None
    if caches is not None:
        caches = [c.reshape(c.shape[0], c.shape[1], -1, c.shape[-1]) for c in caches]
    new_b, new_wkv, new_shift = [], [], []
    h = rmsnorm_rows(x.reshape(m, d), p["norm_gains"][0, 0])
    for layer in range(depth):
        gains = p["norm_gains"][layer]
        li = layer // 2
        last = layer == depth - 1
        next_pre = p["norm_gains"][layer + 1, 0] if not last else gains[0]
        if layer % 2 == 0:
            mix_in, kv_bufs, s_b = _ab_layer(x, h, li, p, caches, b_states, kv_bufs)
            new_b.append(s_b)
            x2, h2 = matmul_post(mix_in, p["w_out_ab"], li, gains[1], x.reshape(m, d), gains[2])
        else:
            wkv0 = jnp.zeros((bn, d // C_HEAD_DIM, C_HEAD_DIM, C_HEAD_DIM), F32) if c_wkv is None else c_wkv[li]
            sh0 = jnp.zeros((bn, d), F32) if c_shift is None else c_shift[li]
            o, gate, s_wkv, s_shift = _rwkv_layer(x, li, gains[0], p, sh0, wkv0)
            new_wkv.append(s_wkv)
            new_shift.append(s_shift)
            if _tiles_time(t_len):
                tb = (bn, t_len)
            else:
                tb = None
                o, gate = (jnp.swapaxes(t.reshape(bn, d, t_len), 1, 2).reshape(m, d) for t in (o, gate))
            x2, h2 = matmul_post(o, p["c_w_out"], li, gains[1], x.reshape(m, d), gains[2], gate=gate, tb=tb)
        next_even = (layer + 1) % 2 == 0 and not last
        x3, h = ffn(h2, p["w_ffn_up"], p["w_ffn_down"], layer, gains[3], x2, next_pre, emit_h=next_even)
        x = x3.reshape(bn, t_len, d)
    a_states = tuple(kv.reshape(kv.shape[0], bn, -1, 2, A_HEADS, A_HEAD_DIM) for kv in kv_bufs)
    return x, a_states, jnp.stack(new_b), jnp.stack(new_wkv), jnp.stack(new_shift)


def kernel(x_prompt, x_sample, cache_a1_kv, cache_a2_kv, cache_a3_kv, state_b, state_c_wkv, state_c_shift, norm_gains, w_in_ab, w_out_ab, b_lower_bounds, b_norm_gain, c_mu, c_w_rkv, c_w0, c_w1, c_w2, c_a0, c_a1, c_a2, c_g1, c_g2, c_k_k, c_k_a, c_r_k, c_ln_w, c_ln_b, c_w_out, w_ffn_up, w_ffn_down):
    p = _prep_params(dict(
        norm_gains=norm_gains, w_in_ab=w_in_ab, w_out_ab=w_out_ab, b_lower_bounds=b_lower_bounds,
        b_norm_gain=b_norm_gain, c_mu=c_mu, c_w_rkv=c_w_rkv, c_w0=c_w0, c_w1=c_w1, c_w2=c_w2, c_a0=c_a0,
        c_a1=c_a1, c_a2=c_a2, c_g1=c_g1, c_g2=c_g2, c_k_k=c_k_k, c_k_a=c_k_a, c_r_k=c_r_k, c_ln_w=c_ln_w,
        c_ln_b=c_ln_b, c_w_out=c_w_out, w_ffn_up=w_ffn_up, w_ffn_down=w_ffn_down))
    y_p, (pa1, pa2, pa3), pb, pwkv, pshift = _trunk(x_prompt, None, None, None, None, p)
    y_s, (sa1, sa2, sa3), sb, swkv, sshift = _trunk(
        x_sample, (cache_a1_kv, cache_a2_kv, cache_a3_kv), state_b, state_c_wkv, state_c_shift, p)
    return (y_p, y_s, pa1, pa2, pa3, pb, pwkv, pshift, sa1, sa2, sa3, sb, swkv, sshift)
```

```python
import functools

import jax
import jax.numpy as jnp
from jax import lax
from jax.experimental import pallas as pl
from jax.experimental.pallas import tpu as pltpu

F32 = jnp.float32
BF16 = jnp.bfloat16

NORM_EPS = 1e-6
A_GROUPS = ((128, 1), (512, 4), (2048, 16))
A_HEADS = 8
A_HEAD_DIM = 128
A_WIDTH = A_HEADS * A_HEAD_DIM
A_BLK = 128
B_HEADS = 8
B_DIM = 128
B_CHUNK = 128
B_SUB = 8
B_HEADS_PER_STEP = 8
C_HEAD_DIM = 64
C_GN_EPS = 64e-5
LANES = 128
MIB = 1024 * 1024


def _cparams(sem, vmem_mib):
    return pltpu.CompilerParams(dimension_semantics=sem, vmem_limit_bytes=vmem_mib * MIB)


def _rms(x, g):
    return x * lax.rsqrt(jnp.mean(x * x, axis=-1, keepdims=True) + NORM_EPS) * g


def _dot(a, b):
    return jnp.dot(a, b, preferred_element_type=F32)


def _dot_nt(a, b):
    return lax.dot_general(a, b, (((1,), (1,)), ((), ())), preferred_element_type=F32)


def _dot_tn(a, b):
    return lax.dot_general(a, b, (((0,), (0,)), ((), ())), preferred_element_type=F32)


def _log2(n):
    assert n > 0 and n & (n - 1) == 0, n
    return n.bit_length() - 1


def _sigmoid(x):
    return 1.0 / (1.0 + jnp.exp(-x))


def _silu(x):
    return x * _sigmoid(x)


def _norm_kernel(x_ref, g_ref, o_ref):
    o_ref[...] = _rms(x_ref[...], g_ref[...]).astype(o_ref.dtype)


def rmsnorm_rows(x, g):
    m, d = x.shape
    tm = min(m, 512)
    return pl.pallas_call(
        _norm_kernel,
        grid=(m // tm,),
        in_specs=[pl.BlockSpec((tm, d), lambda i: (i, 0)),
                  pl.BlockSpec((1, d), lambda i: (0, 0))],
        out_specs=pl.BlockSpec((tm, d), lambda i: (i, 0)),
        out_shape=jax.ShapeDtypeStruct((m, d), BF16),
        compiler_params=_cparams(("parallel",), 32),
        name="rmsnorm_rows",
    )(x, g.reshape(1, d))


def _rwkv_decay(z):
    y = -z
    softplus = jnp.maximum(y, 0.0) + jnp.log(1.0 + jnp.exp(-jnp.abs(y)))
    return jnp.exp(-jnp.exp(-softplus - 0.5))


def _mm_kernel(*refs, act, has_bias, w_t, out_t):
    if has_bias:
        a_ref, w_ref, b_ref, o_ref = refs
    else:
        a_ref, w_ref, o_ref = refs
    w = w_ref[...].astype(BF16)
    if out_t:
        y = _dot_nt(w, a_ref[...])
    elif w_t:
        y = _dot_nt(a_ref[...], w)
    else:
        y = _dot(a_ref[...], w)
    if has_bias:
        y = y + b_ref[...]
    if act == "tanh":
        y = jnp.tanh(y)
    elif act == "sigmoid":
        y = _sigmoid(y)
    elif act == "rwkv_decay":
        y = _rwkv_decay(y)
    o_ref[...] = y.astype(o_ref.dtype)


MM_BLOCK_BUDGET = 40 * MIB
MM_VMEM_LIMIT_MIB = 52


def _mm_tiles(rows, k, n, w_itemsize, out_itemsize):
    def fits(tm, tn):
        return 2 * (tm * k * 2 + k * tn * w_itemsize + tm * tn * out_itemsize) <= MM_BLOCK_BUDGET

    tms = [t for t in (2048, 1024, 512, 256, 128, 64, 32, 16) if t <= rows and rows % t == 0] or [rows]
    tns = [t for t in (2048, 1024, 512, 256, 128) if t <= n and n % t == 0] or [n]
    if n <= tns[0] and 2 * k * n * w_itemsize <= MM_BLOCK_BUDGET // 2:
        for tm in tms:
            if fits(tm, n):
                return tm, n
    for tm in tms:
        for tn in tns:
            if fits(tm, tn):
                return tm, tn
    return tms[-1], tns[-1]


def matmul(a, w, widx=(), bias=None, act=None, out_dtype=F32, tb=None, w_t=False):
    m, k = a.shape
    n = w.shape[-2] if w_t else w.shape[-1]
    assert (w.shape[-1] if w_t else w.shape[-2]) == k and (tb is None or w_t)
    tm, tn = _mm_tiles(m if tb is None else tb[1], k, n, w.dtype.itemsize, jnp.dtype(out_dtype).itemsize)
    nj = n // tn
    if tb is None:
        out_shape = (m, n)
        out_block = (tm, tn)
        out_index = lambda i, j: (i, j)
    else:
        bn, t_len = tb
        nt = t_len // tm
        out_shape = (bn * n, t_len)
        out_block = (tn, tm)
        out_index = lambda i, j: ((i // nt) * nj + j, i % nt)
    lead = (None,) * len(widx)
    in_specs = [pl.BlockSpec((tm, k), lambda i, j: (i, 0))]
    if w_t:
        in_specs.append(pl.BlockSpec(lead + (tn, k), lambda i, j: widx + (j, 0)))
    else:
        in_specs.append(pl.BlockSpec(lead + (k, tn), lambda i, j: widx + (0, j)))
    args = [a, w]
    if bias is not None:
        if tb is None:
            in_specs.append(pl.BlockSpec((1, tn), lambda i, j: (0, j)))
            args.append(bias.reshape(1, n))
        else:
            in_specs.append(pl.BlockSpec((tn, 1), lambda i, j: (j, 0)))
            args.append(bias.reshape(n, 1))
    return pl.pallas_call(
        functools.partial(_mm_kernel, act=act, has_bias=bias is not None, w_t=w_t, out_t=tb is not None),
        grid=(m // tm, nj),
        in_specs=in_specs,
        out_specs=pl.BlockSpec(out_block, out_index),
        out_shape=jax.ShapeDtypeStruct(out_shape, out_dtype),
        compiler_params=_cparams(("parallel", "parallel"), MM_VMEM_LIMIT_MIB),
        name="matmul",
    )(*args)


def _mm_post_kernel(*refs, has_gate, emit_h, transpose_in):
    refs = list(refs)
    a_ref = refs.pop(0)
    gate_ref = refs.pop(0) if has_gate else None
    w_ref, g1_ref, xres_ref, gn_ref, x_out = refs[:5]
    a = a_ref[...]
    if has_gate:
        a = a * gate_ref[...]
    a = a.astype(BF16)
    y = _dot_tn(a, w_ref[...]) if transpose_in else _dot(a, w_ref[...])
    x_new = xres_ref[...] + _rms(y, g1_ref[...])
    x_out[...] = x_new
    if emit_h:
        refs[5][...] = _rms(x_new, gn_ref[...]).astype(BF16)


def matmul_post(a, w, li, g_post, xres, g_next, gate=None, emit_h=True, tb=None):
    m, d = xres.shape
    k = w.shape[1]
    tm = 256 if gate is not None else 512
    tm = min(m, tm) if tb is None else min(tb[1], tm)
    row = lambda i: (i, 0)
    fixed = lambda i: (0, 0)
    if tb is None:
        a_spec = pl.BlockSpec((tm, k), row)
    else:
        nt = tb[1] // tm
        a_spec = pl.BlockSpec((k, tm), lambda i: (i // nt, i % nt))
    in_specs = [a_spec]
    args = [a]
    if gate is not None:
        in_specs.append(a_spec)
        args.append(gate)
    in_specs += [pl.BlockSpec((None, k, d), lambda i: (li, 0, 0)), pl.BlockSpec((1, d), fixed),
                 pl.BlockSpec((tm, d), row), pl.BlockSpec((1, d), fixed)]
    args += [w, g_post.reshape(1, d), xres, g_next.reshape(1, d)]
    out_specs = [pl.BlockSpec((tm, d), row)]
    out_shape = [jax.ShapeDtypeStruct((m, d), F32)]
    if emit_h:
        out_specs.append(pl.BlockSpec((tm, d), row))
        out_shape.append(jax.ShapeDtypeStruct((m, d), BF16))
    res = pl.pallas_call(
        functools.partial(_mm_post_kernel, has_gate=gate is not None, emit_h=emit_h, transpose_in=tb is not None),
        grid=(m // tm,),
        in_specs=in_specs,
        out_specs=out_specs,
        out_shape=out_shape,
        compiler_params=_cparams(("parallel",), 48),
        name="matmul_post",
    )(*args)
    return (res[0], res[1]) if emit_h else (res[0], None)


def _ffn_kernel(h_ref, wg_ref, wu_ref, wd_ref, g1_ref, xres_ref, gn_ref, *outs, emit_h):
    if emit_h:
        x_out, h_out, acc_ref = outs
    else:
        x_out, acc_ref = outs
    j = pl.program_id(1)

    @pl.when(j == 0)
    def _():
        acc_ref[...] = jnp.zeros_like(acc_ref)

    h = h_ref[...]
    gate = _dot(h, wg_ref[...])
    up = _dot(h, wu_ref[...])
    act = (_silu(gate) * up).astype(BF16)
    acc_ref[...] += _dot(act, wd_ref[...])

    @pl.when(j == pl.num_programs(1) - 1)
    def _():
        x_new = xres_ref[...] + _rms(acc_ref[...], g1_ref[...])
        x_out[...] = x_new
        if emit_h:
            h_out[...] = _rms(x_new, gn_ref[...]).astype(BF16)


def ffn(h, w_up, w_down, layer, g_post, xres, g_next, emit_h=True):
    m, d = h.shape
    f = w_down.shape[1]
    tm = min(m, 512)
    th = 512
    nj = f // th
    row = lambda i, j: (i, 0)
    fixed = lambda i, j: (0, 0)
    in_specs = [pl.BlockSpec((tm, d), row),
                pl.BlockSpec((None, d, th), lambda i, j: (layer, 0, j)),
                pl.BlockSpec((None, d, th), lambda i, j: (layer, 0, j + nj)),
                pl.BlockSpec((None, th, d), lambda i, j: (layer, j, 0)),
                pl.BlockSpec((1, d), fixed),
                pl.BlockSpec((tm, d), row),
                pl.BlockSpec((1, d), fixed)]
    out_specs = [pl.BlockSpec((tm, d), row)]
    out_shape = [jax.ShapeDtypeStruct((m, d), F32)]
    if emit_h:
        out_specs.append(pl.BlockSpec((tm, d), row))
        out_shape.append(jax.ShapeDtypeStruct((m, d), BF16))
    res = pl.pallas_call(
        functools.partial(_ffn_kernel, emit_h=emit_h),
        grid=(m // tm, nj),
        in_specs=in_specs,
        out_specs=out_specs,
        out_shape=out_shape,
        scratch_shapes=[pltpu.VMEM((tm, d), F32)],
        compiler_params=_cparams(("parallel", "arbitrary"), 52),
        name="ffn",
    )(h, w_up, w_up, w_down, g_post.reshape(1, d), xres, g_next.reshape(1, d))
    return (res[0], res[1]) if emit_h else (res[0], None)


def _alibi_slope(shape):
    h = pl.program_id(1)
    expo = (h + 1).astype(F32) * (-8.0 / A_HEADS)
    return jnp.exp2(jnp.full(shape, 1.0, F32) * expo)


def _attn_prompt_kernel(q1, k1, v1, q2, k2, v2, q3, k3, v3, out_ref, o_scr, l_scr):
    t_len = out_ref.shape[0]
    blk = A_BLK
    scale = A_HEAD_DIM ** -0.5
    slope = _alibi_slope((blk, blk))
    qi = lax.broadcasted_iota(jnp.int32, (blk, blk), 0)
    ki = lax.broadcasted_iota(jnp.int32, (blk, blk), 1)
    dist_cur = (qi - ki).astype(F32)
    dist_prev = (qi - ki + blk).astype(F32)
    mask_cur = ki <= qi
    mask_prev = ki >= qi
    neg_inf = jnp.float32(-jnp.inf)

    def rows(start, dil):
        if dil == 1:
            return pl.ds(pl.multiple_of(start, blk), blk)
        return pl.ds(start, blk, stride=dil)

    def one_block(gi, q_ref, k_ref, v_ref, dil, start, prev_start, use_prev):
        q = q_ref[rows(start, dil), :].astype(BF16)
        kc = k_ref[rows(start, dil), :].astype(BF16)
        vc = v_ref[rows(start, dil), :].astype(BF16)
        s_c = _dot_nt(q, kc) * scale - slope * (dist_cur * float(dil))
        s_c = jnp.where(mask_cur, s_c, neg_inf)
        mx = jnp.max(s_c, axis=-1, keepdims=True)
        if use_prev is not None:
            kp = k_ref[rows(prev_start, dil), :].astype(BF16)
            vp = v_ref[rows(prev_start, dil), :].astype(BF16)
            s_p = _dot_nt(q, kp) * scale - slope * (dist_prev * float(dil))
            s_p = jnp.where(mask_prev, s_p + jnp.where(use_prev, 0.0, neg_inf), neg_inf)
            mx = jnp.maximum(mx, jnp.max(s_p, axis=-1, keepdims=True))
        p_c = jnp.exp(s_c - mx)
        den = jnp.sum(p_c, axis=-1, keepdims=True)
        acc = _dot(p_c.astype(BF16), vc)
        if use_prev is not None:
            p_p = jnp.exp(s_p - mx)
            den = den + jnp.sum(p_p, axis=-1, keepdims=True)
            acc = acc + _dot(p_p.astype(BF16), vp)
        o = acc / den
        lse = mx + jnp.log(den)
        o_scr[gi, rows(start, dil), :] = o
        l_scr[gi, rows(start, dil), :] = jnp.broadcast_to(lse, (blk, LANES))

    for gi, (q_ref, k_ref, v_ref) in enumerate(((q1, k1, v1), (q2, k2, v2), (q3, k3, v3))):
        window, dil = A_GROUPS[gi]
        assert window // dil == blk
        m_len = t_len // dil
        nb = m_len // blk
        assert nb * blk * dil == t_len

        def body(idx, carry, gi=gi, q_ref=q_ref, k_ref=k_ref, v_ref=v_ref, dil=dil, nb=nb):
            r = idx // nb
            n = idx - r * nb
            start = n * (blk * dil) + r
            if nb == 1:
                one_block(gi, q_ref, k_ref, v_ref, dil, start, None, None)
            else:
                prev_start = jnp.maximum(n - 1, 0) * (blk * dil) + r
                one_block(gi, q_ref, k_ref, v_ref, dil, start, prev_start, n > 0)
            return carry

        lax.fori_loop(0, dil * nb, body, 0, unroll=16)

    def merge(c, carry):
        sl = pl.ds(pl.multiple_of(c * blk, blk), blk)
        l0, l1, l2 = l_scr[0, sl, :], l_scr[1, sl, :], l_scr[2, sl, :]
        mx = jnp.maximum(jnp.maximum(l0, l1), l2)
        w0, w1, w2 = jnp.exp(l0 - mx), jnp.exp(l1 - mx), jnp.exp(l2 - mx)
        num = w0 * o_scr[0, sl, :] + w1 * o_scr[1, sl, :] + w2 * o_scr[2, sl, :]
        out_ref[sl, :] = (num / (w0 + w1 + w2)).astype(out_ref.dtype)
        return carry

    lax.fori_loop(0, t_len // blk, merge, 0, unroll=2)


def attn_prompt(proj):
    bn, t_len, _ = proj.shape
    hd = A_HEAD_DIM

    def col(cb):
        return pl.BlockSpec((None, t_len, hd), lambda b, h, cb=cb: (b, 0, cb * A_HEADS + h))

    in_specs = [col(3 * g + j) for g in range(3) for j in range(3)]
    return pl.pallas_call(
        _attn_prompt_kernel,
        grid=(bn, A_HEADS),
        in_specs=in_specs,
        out_specs=pl.BlockSpec((None, t_len, hd), lambda b, h: (b, 0, h)),
        out_shape=jax.ShapeDtypeStruct((bn, t_len, A_WIDTH), BF16),
        scratch_shapes=[pltpu.VMEM((3, t_len, hd), F32), pltpu.VMEM((3, t_len, LANES), F32)],
        compiler_params=_cparams(("parallel", "parallel"), 40),
        name="attn_prompt",
    )(*([proj] * 9))


KV_ROWS = 2 * A_HEADS
SAMPLE_CHUNKS = 4


def _attn_sample_kernel(*refs, create):
    cols = refs[0:9]
    chunk_refs = refs[9:12]
    next_refs = refs[12:15]
    out_ref = refs[15]
    buf_refs = refs[16:19]
    m_scr, d_scr, acc_scr, new_scr = refs[19:]
    if create:
        c, n_chunks = pl.program_id(2), pl.num_programs(2)
        live = pl.program_id(0) == pl.num_programs(0) - 1
    else:
        c, n_chunks = pl.program_id(1), pl.num_programs(1)
    last = c == n_chunks - 1
    l_new = out_ref.shape[0]
    shift = l_new * KV_ROWS
    hd = A_HEAD_DIM
    scale = hd ** -0.5
    neg_inf = jnp.float32(-jnp.inf)

    nq = A_HEADS * l_new

    @pl.when(c == 0)
    def _():
        m_scr[...] = jnp.full(m_scr.shape, neg_inf, F32)
        d_scr[...] = jnp.zeros(d_scr.shape, F32)
        acc_scr[...] = jnp.zeros(acc_scr.shape, F32)
        for gi in range(3):
            for kv, src in enumerate(cols[3 * gi + 1:3 * gi + 3]):
                for h in range(A_HEADS):
                    new_scr[gi, pl.ds(kv * A_HEADS + h, l_new, stride=KV_ROWS), :] = src[:, h * hd:(h + 1) * hd]
            if not create:
                buf_refs[gi][...] = new_scr[gi]

    if create:
        for gi in range(3):
            c_ref, n_ref, o_ref = chunk_refs[gi], next_refs[gi], buf_refs[gi]
            rows = c_ref.shape[0]
            tail = jnp.where(live, new_scr[gi], jnp.zeros((shift, hd), F32))
            o_ref[0:rows - shift, :] = c_ref[shift:rows, :]
            o_ref[rows - shift:rows, :] = jnp.where(last, tail, n_ref[...])

    def segment(q_all, buf, pos0, n_buf, window, dil, live):
        rows = buf.shape[0]
        keys_on_value_rows = jnp.concatenate([jnp.zeros((A_HEADS, hd), F32), buf[0:rows - A_HEADS]], axis=0)
        s = _dot_nt(q_all, keys_on_value_rows.astype(BF16))
        qrow = lax.broadcasted_iota(jnp.int32, (nq, rows), 0)
        col = lax.broadcasted_iota(jnp.int32, (nq, rows), 1)
        head = qrow >> _log2(l_new)
        dist = n_buf + (qrow & (l_new - 1)) - (pos0 + (col >> _log2(KV_ROWS)))
        valid = jnp.logical_and(
            jnp.logical_and((col & (KV_ROWS - 1)) == A_HEADS + head, (dist & (dil - 1)) == 0),
            jnp.logical_and(dist >= 0, dist <= window))
        slope = jnp.exp2((head + 1).astype(F32) * (-8.0 / A_HEADS))
        s = jnp.where(valid, s * scale - slope * dist.astype(F32), neg_inf)
        if live is not None:
            s = jnp.where(live, s, neg_inf)
        m_old = m_scr[...]
        m_new = jnp.maximum(m_old, jnp.max(s, axis=-1, keepdims=True))
        alpha = jnp.exp(m_old - m_new)
        p = jnp.exp(s - m_new[:, 0:1])
        d_scr[...] = alpha * d_scr[...] + jnp.sum(p, axis=-1, keepdims=True)
        acc_scr[...] = alpha * acc_scr[...] + _dot(p.astype(BF16), buf.astype(BF16))
        m_scr[...] = m_new

    def attend():
        for gi in range(3):
            window, dil = A_GROUPS[gi]
            q_ref = cols[3 * gi]
            c_ref = chunk_refs[gi]
            p_len = c_ref.shape[0] // KV_ROWS
            n_buf = p_len * SAMPLE_CHUNKS
            q_all = jnp.concatenate([q_ref[:, h * hd:(h + 1) * hd] for h in range(A_HEADS)],
                                    axis=0).astype(BF16)
            segment(q_all, new_scr[gi], n_buf, n_buf, window, dil, c == 0)
            segment(q_all, c_ref[...], c * p_len, n_buf, window, dil, None)

    if create:
        pl.when(live)(attend)
    else:
        attend()

    @pl.when(last)
    def _():
        o = acc_scr[...] / d_scr[...]
        if create:
            o = jnp.where(live, o, 0.0)
        for h in range(A_HEADS):
            out_ref[:, h * hd:(h + 1) * hd] = o[h * l_new:(h + 1) * l_new]


def attn_sample(proj, caches, li, bufs):
    bn, l_new, _ = proj.shape
    hd = A_HEAD_DIM
    shift = l_new * KV_ROWS
    n_ab = caches[0].shape[0]
    create = bufs is None
    if create:
        ids = lambda s, b, c: ((li + 1 + s) % n_ab, b, c)
        grid = (n_ab, bn, SAMPLE_CHUNKS)
        sem = ("arbitrary", "arbitrary", "arbitrary")
    else:
        ids = lambda b, c: (li, b, c)
        grid = (bn, SAMPLE_CHUNKS)
        sem = ("parallel", "arbitrary")

    def spec(block, index):
        return pl.BlockSpec(block, lambda *g: index(*ids(*g)))

    in_specs = [spec((None, l_new, A_WIDTH), lambda s, b, c, cb=3 * g + j: (b, 0, cb))
                for g in range(3) for j in range(3)]
    args = [proj] * 9
    chunk_specs, next_specs, tail_specs, out_shape = [], [], [], []
    for g in range(3):
        rows = caches[g].shape[2] // SAMPLE_CHUNKS
        assert rows % shift == 0 and rows > shift
        per = rows // shift
        n_blk = caches[g].shape[2] // shift
        chunk_specs.append(spec((None, None, rows, hd), lambda s, b, c: (s, b, c, 0)))
        next_specs.append(spec((None, None, shift, hd),
                               lambda s, b, c, per=per, n_blk=n_blk: (s, b, jnp.minimum((c + 1) * per, n_blk - 1), 0)))
        tail_specs.append(spec((None, None, shift, hd), lambda s, b, c, n_blk=n_blk: (s, b, n_blk - 1, 0)))
        out_shape.append(jax.ShapeDtypeStruct(caches[g].shape, F32))
    if create:
        in_specs += chunk_specs + next_specs
        args += list(caches) + list(caches)
        buf_specs, aliases = chunk_specs, {}
    else:
        in_specs += chunk_specs + [pl.BlockSpec(memory_space=pl.ANY)] * 3
        args += list(caches) + list(bufs)
        buf_specs, aliases = tail_specs, {12 + g: 1 + g for g in range(3)}
    if create:
        o_spec = pl.BlockSpec((None, None, l_new, A_WIDTH), lambda s, b, c: (s, b, 0, 0))
        o_shape = jax.ShapeDtypeStruct((n_ab, bn, l_new, A_WIDTH), F32)
    else:
        o_spec = pl.BlockSpec((None, l_new, A_WIDTH), lambda b, c: (b, 0, 0))
        o_shape = jax.ShapeDtypeStruct((bn, l_new, A_WIDTH), F32)
    res = pl.pallas_call(
        functools.partial(_attn_sample_kernel, create=create),
        grid=grid,
        in_specs=in_specs,
        out_specs=[o_spec] + buf_specs,
        out_shape=[o_shape] + out_shape,
        scratch_shapes=[pltpu.VMEM((A_HEADS * l_new, LANES), F32)] * 3 + [pltpu.VMEM((3, shift, hd), F32)],
        input_output_aliases=aliases,
        compiler_params=_cparams(sem, 48),
        name="attn_sample",
    )(*args)
    return (res[0][n_ab - 1] if create else res[0]), list(res[1:])


def _kv_pack_kernel(*refs, li):
    k_ref, v_ref, o_ref = refs[0], refs[1], refs[-1]
    tm = k_ref.shape[0]
    hd = A_HEAD_DIM
    slab = pl.program_id(0)

    @pl.when(slab == li)
    def _():
        for kv, src in enumerate((k_ref, v_ref)):
            for h in range(A_HEADS):
                o_ref[pl.ds(kv * A_HEADS + h, tm, stride=KV_ROWS), :] = src[:, h * hd:(h + 1) * hd]

    @pl.when(slab != li)
    def _():
        o_ref[...] = jnp.zeros_like(o_ref)


def kv_pack(proj, g, li, n_ab, prev):
    bn, t_len, _ = proj.shape
    hd = A_HEAD_DIM
    w = min(A_GROUPS[g][0], t_len)
    tm = min(w, 128)
    row0 = (t_len - w) // tm
    in_specs = [pl.BlockSpec((None, tm, A_WIDTH), lambda s, b, i: (b, row0 + i, 3 * g + 1)),
                pl.BlockSpec((None, tm, A_WIDTH), lambda s, b, i: (b, row0 + i, 3 * g + 2))]
    args = [proj, proj]
    aliases = {}
    if prev is None:
        slabs, first = n_ab, 0
    else:
        slabs, first = 1, li
        aliases[2] = 0
        in_specs.append(pl.BlockSpec(memory_space=pl.ANY))
        args.append(prev)
    return pl.pallas_call(
        functools.partial(_kv_pack_kernel, li=li - first),
        grid=(slabs, bn, w // tm),
        in_specs=in_specs,
        out_specs=pl.BlockSpec((None, None, tm * KV_ROWS, hd), lambda s, b, i: (first + s, b, i, 0)),
        out_shape=jax.ShapeDtypeStruct((n_ab, bn, w * KV_ROWS, hd), F32),
        input_output_aliases=aliases,
        compiler_params=_cparams(("parallel", "parallel", "parallel"), 32),
        name="kv_pack",
    )(*args)


def _gla_kernel(bq_ref, bf_ref, bi_ref, bg_ref, lbp_ref, gain_ref, s0_ref,
                o_ref, sfin_ref, st_scr, att_scr, qkb_scr, *, li, cin):
    c = pl.program_id(2)
    ch = B_CHUNK
    last = c == pl.num_programs(2) - 1

    def pad(x):
        if cin == ch:
            return x
        return jnp.concatenate([x, jnp.zeros((ch - cin, x.shape[1]), x.dtype)], axis=0)

    row = lax.broadcasted_iota(jnp.int32, (ch, B_DIM), 0)
    lane = lax.broadcasted_iota(jnp.int32, (B_SUB, ch), 1)
    row_sub = lax.broadcasted_iota(jnp.int32, (B_SUB, B_DIM), 0)
    live = row < cin
    tri = (row >= lax.broadcasted_iota(jnp.int32, (ch, ch), 1)).astype(F32)

    @pl.when(c == 0)
    def _():
        for hj in range(B_HEADS_PER_STEP):
            st_scr[hj] = s0_ref[hj].T

    for hj in range(B_HEADS_PER_STEP):
        hs = slice(hj * B_DIM, (hj + 1) * B_DIM)
        lbp = lbp_ref[:, hs]
        e = jnp.exp(lbp - jnp.max(lbp, axis=0, keepdims=True))
        soft = e / jnp.sum(e, axis=0, keepdims=True)
        lb = jnp.sum(soft[0:li + 1], axis=0, keepdims=True) - soft[0:1]

        q = _silu(pad(bq_ref[:, hs]))
        fgate = lb + (1.0 - lb) * _sigmoid(pad(bf_ref[:, hs]))
        g = jnp.where(live, jnp.log(fgate), 0.0)
        kk = jnp.where(live, 1.0 - fgate, 0.0)
        v16 = pad(bi_ref[:, hs]).astype(BF16)

        b = jnp.dot(tri, g, preferred_element_type=F32, precision=lax.Precision.HIGHEST)
        b_last = b[ch - 1:ch]
        st = st_scr[hj]

        o = _dot_nt((q * jnp.exp(b)).astype(BF16), st.astype(BF16))

        q_scr, kk_scr, b_scr = (qkb_scr.at[hj, j] for j in range(3))
        q_scr[...] = q
        kk_scr[...] = kk
        b_scr[...] = b

        for qb in range(ch // B_SUB):
            r0 = qb * B_SUB
            q_s = q_scr[r0:r0 + B_SUB, :]
            b_s = b_scr[r0:r0 + B_SUB, :]
            kk_s = kk_scr[r0:r0 + B_SUB, :]
            if qb > 0:
                b_ref = b_scr[r0 - 1:r0, :]
                qf = q_s * jnp.exp(b_s - b_ref)
                earlier = row < r0
                kf = jnp.where(earlier, kk_scr[...] * jnp.exp(jnp.where(earlier, b_ref - b_scr[...], 0.0)), 0.0)
                att = _dot_nt(qf.astype(BF16), kf.astype(BF16))
            else:
                att = jnp.zeros((B_SUB, ch), F32)
            for s in range(B_SUB):
                causal = row_sub >= s
                dec = jnp.where(causal, jnp.exp(jnp.where(causal, b_s - b_s[s:s + 1], 0.0)), 0.0)
                colv = jnp.sum(q_s * dec * kk_s[s:s + 1], axis=-1, keepdims=True)
                att = jnp.where(lane == r0 + s, colv, att)
            att_scr[hj, r0:r0 + B_SUB, :] = att

        o = o + _dot(att_scr[hj].astype(BF16), v16)

        kd = kk * jnp.exp(b_last - b)
        st_new = st * jnp.exp(b_last) + _dot_tn(v16, kd.astype(BF16))
        st_scr[hj] = st_new

        on = _rms(o, gain_ref[:, hs]) * _silu(pad(bg_ref[:, hs]))
        o_ref[:, hs] = on[0:cin].astype(o_ref.dtype)

    @pl.when(last)
    def _():
        for hj in range(B_HEADS_PER_STEP):
            sfin_ref[hj] = st_scr[hj].T


def gla(proj, lb_params, b_gain, s0, li):
    bn, t_len, ab_in = proj.shape
    n_ab = lb_params.shape[0]
    cin = min(B_CHUNK, t_len)
    nc = t_len // cin
    base = ab_in // B_DIM - 4 * B_HEADS

    hp = B_HEADS_PER_STEP
    wid = hp * B_DIM
    base = base // hp

    def col(j):
        return pl.BlockSpec((None, cin, wid), lambda b, h, c, j=j: (b, c, base + j * (B_HEADS // hp) + h))

    out_dtype = F32 if cin < 16 else BF16
    return pl.pallas_call(
        functools.partial(_gla_kernel, li=li, cin=cin),
        grid=(bn, B_HEADS // hp, nc),
        in_specs=[col(0), col(1), col(2), col(3),
                  pl.BlockSpec((n_ab, wid), lambda b, h, c: (0, h)),
                  pl.BlockSpec((None, 1, wid), lambda b, h, c: (li, 0, h)),
                  pl.BlockSpec((None, hp, B_DIM, B_DIM), lambda b, h, c: (b, h, 0, 0))],
        out_specs=[pl.BlockSpec((None, cin, wid), lambda b, h, c: (b, c, h)),
                   pl.BlockSpec((None, hp, B_DIM, B_DIM), lambda b, h, c: (b, h, 0, 0))],
        out_shape=[jax.ShapeDtypeStruct((bn, t_len, B_HEADS * B_DIM), out_dtype),
                   jax.ShapeDtypeStruct((bn, B_HEADS, B_DIM, B_DIM), F32)],
        scratch_shapes=[pltpu.VMEM((hp, B_DIM, B_DIM), F32), pltpu.VMEM((hp, B_CHUNK, B_CHUNK), F32),
                        pltpu.VMEM((hp, 3, B_CHUNK, B_DIM), F32)],
        compiler_params=_cparams(("parallel", "parallel", "arbitrary"), 32),
        name="gla",
    )(proj, proj, proj, proj, lb_params, b_gain.reshape(n_ab, 1, -1), s0)


def _mix_kernel(x_ref, xp_ref, sh_ref, g_ref, mu_ref, *outs):
    xm_refs, hl_ref = outs[:6], outs[6]
    i = pl.program_id(1)
    tt = x_ref.shape[0]
    g = g_ref[...]
    h = _rms(x_ref[...], g)
    hp = _rms(xp_ref[...], g)
    first = jnp.where(i == 0, sh_ref[...], hp[7:8])
    rolled = pltpu.roll(h, 1, axis=0)
    row = lax.broadcasted_iota(jnp.int32, h.shape, 0)
    prev = jnp.where(row == 0, first, rolled)
    xx = prev - h
    for j in range(6):
        xm_refs[j][...] = (h + xx * mu_ref[j:j + 1]).astype(BF16)

    @pl.when(i == pl.num_programs(1) - 1)
    def _():
        hl_ref[...] = h[tt - 1:tt]


def rwkv_mix(x, shift, g, mu):
    bn, t_len, d = x.shape
    tt = min(t_len, 256)
    sub = tt // 8
    tile = pl.BlockSpec((None, tt, d), lambda b, i: (b, i, 0))
    *xm, hl = pl.pallas_call(
        _mix_kernel,
        grid=(bn, t_len // tt),
        in_specs=[pl.BlockSpec((None, tt, d), lambda b, i: (b, i, 0)),
                  pl.BlockSpec((None, 8, d), lambda b, i: (b, jnp.maximum(i * sub - 1, 0), 0)),
                  pl.BlockSpec((None, 1, d), lambda b, i: (b, 0, 0)),
                  pl.BlockSpec((1, d), lambda b, i: (0, 0)),
                  pl.BlockSpec((6, d), lambda b, i: (0, 0))],
        out_specs=[tile] * 6 + [pl.BlockSpec((None, 1, d), lambda b, i: (b, 0, 0))],
        out_shape=[jax.ShapeDtypeStruct((bn, t_len, d), BF16)] * 6 + [jax.ShapeDtypeStruct((bn, 1, d), F32)],
        compiler_params=_cparams(("parallel", "arbitrary"), 40),
        name="rwkv_mix",
    )(x, x, shift.reshape(bn, 1, d), g.reshape(1, d), mu)
    return [t.reshape(bn * t_len, d) for t in xm], hl.reshape(bn, d)


def _scan_kernel(r_ref, k_ref, v_ref, w_ref, a_ref, par_ref, s0_ref,
                 o_ref, sfin_ref, st, ops, *, tc):
    c = pl.program_id(1)
    n = C_HEAD_DIM

    @pl.when(c == 0)
    def _():
        st[...] = s0_ref[...]

    def key_row(j, ki):
        return ops[j, pl.ds(ki, 1), :]

    def step(t, carry):
        c_kk, c_ka, c_rk, ln_w, ln_b = (par_ref[j] for j in range(5))
        w = w_ref[t]
        a = a_ref[t]
        kt = k_ref[t]
        kk = kt * c_kk
        nrm = jnp.sqrt(jnp.sum(kk * kk, axis=0, keepdims=True))
        kk = kk / jnp.maximum(nrm, 1e-12)
        k2 = kt * (1.0 + (a - 1.0) * c_ka)
        rt = r_ref[t]
        vt = v_ref[t]
        ops[0] = -kk
        ops[1] = w
        ops[2] = kk * a
        ops[3] = k2
        ops[4] = rt
        sa0 = st[0] * key_row(0, 0)
        sa1 = st[1] * key_row(0, 1)
        for ki in range(2, n, 2):
            sa0 = sa0 + st[ki] * key_row(0, ki)
            sa1 = sa1 + st[ki + 1] * key_row(0, ki + 1)
        sa = sa0 + sa1
        o0 = o1 = None
        for ki in range(n):
            s_n = st[ki] * key_row(1, ki) + sa * key_row(2, ki) + vt * key_row(3, ki)
            st[ki] = s_n
            term = s_n * key_row(4, ki)
            if ki % 2 == 0:
                o0 = term if o0 is None else o0 + term
            else:
                o1 = term if o1 is None else o1 + term
        o = o0 + o1
        mean = jnp.mean(o, axis=0, keepdims=True)
        var = jnp.mean(jnp.square(o - mean), axis=0, keepdims=True)
        o = (o - mean) * lax.rsqrt(var + C_GN_EPS) * ln_w + ln_b
        bonus = jnp.sum(rt * k2 * c_rk, axis=0, keepdims=True)
        o_ref[t] = o + bonus * vt
        return carry

    lax.fori_loop(0, tc, step, 0)

    @pl.when(c == pl.num_programs(1) - 1)
    def _():
        sfin_ref[...] = st[...]


def rwkv_scan(r, k, v, decay, iclr, par, s0):
    t_len, n, bh = r.shape
    tc = min(t_len, 32)
    seq = pl.BlockSpec((tc, n, LANES), lambda l, c: (c, 0, l))
    return pl.pallas_call(
        functools.partial(_scan_kernel, tc=tc),
        grid=(bh // LANES, t_len // tc),
        in_specs=[seq] * 5 + [pl.BlockSpec((5, n, LANES), lambda l, c: (0, 0, l)),
                              pl.BlockSpec((n, n, LANES), lambda l, c: (0, 0, l))],
        out_specs=[seq, pl.BlockSpec((n, n, LANES), lambda l, c: (0, 0, l))],
        out_shape=[jax.ShapeDtypeStruct((t_len, n, bh), F32),
                   jax.ShapeDtypeStruct((n, n, bh), F32)],
        scratch_shapes=[pltpu.VMEM((n, n, LANES), F32), pltpu.VMEM((5, n, LANES), F32)],
        compiler_params=_cparams(("parallel", "arbitrary"), 40),
        name="rwkv_scan",
    )(r, k, v, decay, iclr, par, s0)


def _tiles_time(t_len):
    return t_len % 256 == 0


def _pad_to(x, axis, mult):
    pad = (-x.shape[axis]) % mult
    if pad == 0:
        return x
    widths = [(0, 0)] * x.ndim
    widths[axis] = (0, pad)
    return jnp.pad(x, widths)


def _ab_layer(x, h, li, p, caches, b_state, prev_bufs):
    bn, t_len, d = x.shape
    n_ab = p["w_in_ab"].shape[0]
    proj = matmul(h, p["w_in_ab"], (li,)).reshape(bn, t_len, -1)
    if caches is None:
        a_out = attn_prompt(proj)
        new_bufs = [kv_pack(proj, g, li, n_ab, None if prev_bufs is None else prev_bufs[g]) for g in range(3)]
        s0 = jnp.zeros((bn, B_HEADS, B_DIM, B_DIM), F32)
    else:
        a_out, new_bufs = attn_sample(proj, caches, li, prev_bufs)
        a_out = a_out.astype(BF16)
        s0 = b_state[li]
    b_out, s_b = gla(proj, p["b_lower_bounds"], p["b_norm_gain"], s0, li)
    mix_in = jnp.concatenate([a_out, b_out.astype(BF16)], axis=-1).reshape(bn * t_len, -1)
    return mix_in, new_bufs, s_b


def _rwkv_layer(x, li, g_pre, p, shift, wkv):
    bn, t_len, d = x.shape
    n = C_HEAD_DIM
    hh = d // n
    (xr, xw, xk, xv, xa, xg), h_last = rwkv_mix(x, shift, g_pre, p["c_mu"][li])

    def proj_tm(a, w_t, widx, bias=None, act=None):
        if _tiles_time(t_len):
            return matmul(a, w_t, widx, bias=bias, act=act, tb=(bn, t_len), w_t=True)
        y = matmul(a, w_t, widx, bias=bias, act=act, w_t=True)
        return jnp.swapaxes(y.reshape(bn, t_len, -1), 1, 2).reshape(-1, t_len)

    r = proj_tm(xr, p["c_w_rkv_t"], (li, 0))
    k = proj_tm(xk, p["c_w_rkv_t"], (li, 1))
    v = proj_tm(xv, p["c_w_rkv_t"], (li, 2))
    w_lo = matmul(xw, p["c_w1"], (li,), act="tanh", out_dtype=BF16)
    decay = proj_tm(w_lo, p["c_w2_t"], (li,), bias=p["c_w0"][li], act="rwkv_decay")
    a_lo = matmul(xa, p["c_a1"], (li,), out_dtype=BF16)
    iclr = proj_tm(a_lo, p["c_a2_t"], (li,), bias=p["c_a0"][li], act="sigmoid")
    g_lo = matmul(xg, p["c_g1"], (li,), act="sigmoid", out_dtype=BF16)
    gate = proj_tm(g_lo, p["c_g2_t"], (li,))

    def to_scan(t):
        return jnp.transpose(t.reshape(bn * hh, n, t_len), (2, 1, 0))

    def lanes(vec):
        return jnp.tile(vec.reshape(hh, n).T, (1, bn))

    par = jnp.stack([lanes(p["c_k_k"][li]), lanes(p["c_k_a"][li]), lanes(p["c_r_k"][li].reshape(-1)),
                     lanes(p["c_ln_w"][li]), lanes(p["c_ln_b"][li])])
    s0 = jnp.transpose(wkv, (3, 2, 0, 1)).reshape(n, n, bn * hh)
    o, s_fin = rwkv_scan(*(to_scan(t) for t in (r, k, v, decay, iclr)), par, s0)
    o = jnp.transpose(o, (2, 1, 0)).reshape(bn * d, t_len)
    s_fin = jnp.transpose(s_fin.reshape(n, n, bn, hh), (2, 3, 1, 0))
    return o, gate, s_fin, h_last


def _prep_params(raw):
    to16 = lambda w: w.astype(BF16)

    def to16_t(w):
        return jnp.swapaxes(_pad_to(w, w.ndim - 2, LANES), -1, -2).astype(BF16)

    p = dict(raw)
    for name in ("w_out_ab", "c_w_out", "w_ffn_up", "w_ffn_down"):
        p[name] = to16(raw[name])
    for name in ("c_w_rkv", "c_w2", "c_a2", "c_g2"):
        p[name + "_t"] = to16_t(raw[name])
        del p[name]
    for name in ("c_w1", "c_a1", "c_g1"):
        p[name] = _pad_to(raw[name], 2, LANES)
    return p


def _trunk(x, caches, b_states, c_wkv, c_shift, p):
    bn, t_len, d = x.shape
    m = bn * t_len
    depth = p["norm_gains"].shape[0]
    kv_bufs = None
    if caches is not None:
        caches = [c.reshape(c.shape[0], c.shape[1], -1, c.shape[-1]) for c in caches]
    new_b, new_wkv, new_shift = [], [], []
    h = rmsnorm_rows(x.reshape(m, d), p["norm_gains"][0, 0])
    for layer in range(depth):
        gains = p["norm_gains"][layer]
        li = layer // 2
        last = layer == depth - 1
        next_pre = p["norm_gains"][layer + 1, 0] if not last else gains[0]
        if layer % 2 == 0:
            mix_in, kv_bufs, s_b = _ab_layer(x, h, li, p, caches, b_states, kv_bufs)
            new_b.append(s_b)
            x2, h2 = matmul_post(mix_in, p["w_out_ab"], li, gains[1], x.reshape(m, d), gains[2])
        else:
            wkv0 = jnp.zeros((bn, d // C_HEAD_DIM, C_HEAD_DIM, C_HEAD_DIM), F32) if c_wkv is None else c_wkv[li]
            sh0 = jnp.zeros((bn, d), F32) if c_shift is None else c_shift[li]
            o, gate, s_wkv, s_shift = _rwkv_layer(x, li, gains[0], p, sh0, wkv0)
            new_wkv.append(s_wkv)
            new_shift.append(s_shift)
            if _tiles_time(t_len):
                tb = (bn, t_len)
            else:
                tb = None
                o, gate = (jnp.swapaxes(t.reshape(bn, d, t_len), 1, 2).reshape(m, d) for t in (o, gate))
            x2, h2 = matmul_post(o, p["c_w_out"], li, gains[1], x.reshape(m, d), gains[2], gate=gate, tb=tb)
        next_even = (layer + 1) % 2 == 0 and not last
        x3, h = ffn(h2, p["w_ffn_up"], p["w_ffn_down"], layer, gains[3], x2, next_pre, emit_h=next_even)
        x = x3.reshape(bn, t_len, d)
    a_states = tuple(kv.reshape(kv.shape[0], bn, -1, 2, A_HEADS, A_HEAD_DIM) for kv in kv_bufs)
    return x, a_states, jnp.stack(new_b), jnp.stack(new_wkv), jnp.stack(new_shift)


def kernel(x_prompt, x_sample, cache_a1_kv, cache_a2_kv, cache_a3_kv, state_b, state_c_wkv, state_c_shift, norm_gains, w_in_ab, w_out_ab, b_lower_bounds, b_norm_gain, c_mu, c_w_rkv, c_w0, c_w1, c_w2, c_a0, c_a1, c_a2, c_g1, c_g2, c_k_k, c_k_a, c_r_k, c_ln_w, c_ln_b, c_w_out, w_ffn_up, w_ffn_down):
    p = _prep_params(dict(
        norm_gains=norm_gains, w_in_ab=w_in_ab, w_out_ab=w_out_ab, b_lower_bounds=b_lower_bounds,
        b_norm_gain=b_norm_gain, c_mu=c_mu, c_w_rkv=c_w_rkv, c_w0=c_w0, c_w1=c_w1, c_w2=c_w2, c_a0=c_a0,
        c_a1=c_a1, c_a2=c_a2, c_g1=c_g1, c_g2=c_g2, c_k_k=c_k_k, c_k_a=c_k_a, c_r_k=c_r_k, c_ln_w=c_ln_w,
        c_ln_b=c_ln_b, c_w_out=c_w_out, w_ffn_up=w_ffn_up, w_ffn_down=w_ffn_down))
    y_p, (pa1, pa2, pa3), pb, pwkv, pshift = _trunk(x_prompt, None, None, None, None, p)
    y_s, (sa1, sa2, sa3), sb, swkv, sshift = _trunk(
        x_sample, (cache_a1_kv, cache_a2_kv, cache_a3_kv), state_b, state_c_wkv, state_c_shift, p)
    return (y_p, y_s, pa1, pa2, pa3, pb, pwkv, pshift, sa1, sa2, sa3, sb, swkv, sshift)
```

```python
import functools

import jax
import jax.numpy as jnp
from jax import lax
from jax.experimental import pallas as pl
from jax.experimental.pallas import tpu as pltpu

F32 = jnp.float32
BF16 = jnp.bfloat16

NORM_EPS = 1e-6
A_GROUPS = ((128, 1), (512, 4), (2048, 16))
A_HEADS = 8
A_HEAD_DIM = 128
A_WIDTH = A_HEADS * A_HEAD_DIM
A_BLK = 128
B_HEADS = 8
B_DIM = 128
B_CHUNK = 128
B_SUB = 8
B_HEADS_PER_STEP = 8
C_HEAD_DIM = 64
C_GN_EPS = 64e-5
LANES = 128
MIB = 1024 * 1024


def _cparams(sem, vmem_mib):
    return pltpu.CompilerParams(dimension_semantics=sem, vmem_limit_bytes=vmem_mib * MIB)


def _rms(x, g):
    return x * lax.rsqrt(jnp.mean(x * x, axis=-1, keepdims=True) + NORM_EPS) * g


def _dot(a, b):
    return jnp.dot(a, b, preferred_element_type=F32)


def _dot_nt(a, b):
    return lax.dot_general(a, b, (((1,), (1,)), ((), ())), preferred_element_type=F32)


def _dot_tn(a, b):
    return lax.dot_general(a, b, (((0,), (0,)), ((), ())), preferred_element_type=F32)


def _log2(n):
    assert n > 0 and n & (n - 1) == 0, n
    return n.bit_length() - 1


def _sigmoid(x):
    return 1.0 / (1.0 + jnp.exp(-x))


def _silu(x):
    return x * _sigmoid(x)


def _norm_kernel(x_ref, g_ref, o_ref):
    o_ref[...] = _rms(x_ref[...], g_ref[...]).astype(o_ref.dtype)


def rmsnorm_rows(x, g):
    m, d = x.shape
    tm = min(m, 512)
    return pl.pallas_call(
        _norm_kernel,
        grid=(m // tm,),
        in_specs=[pl.BlockSpec((tm, d), lambda i: (i, 0)),
                  pl.BlockSpec((1, d), lambda i: (0, 0))],
        out_specs=pl.BlockSpec((tm, d), lambda i: (i, 0)),
        out_shape=jax.ShapeDtypeStruct((m, d), BF16),
        compiler_params=_cparams(("parallel",), 32),
        name="rmsnorm_rows",
    )(x, g.reshape(1, d))


def _rwkv_decay(z):
    y = -z
    softplus = jnp.maximum(y, 0.0) + jnp.log(1.0 + jnp.exp(-jnp.abs(y)))
    return jnp.exp(-jnp.exp(-softplus - 0.5))


def _mm_kernel(*refs, act, has_bias, w_t, out_t):
    if has_bias:
        a_ref, w_ref, b_ref, o_ref = refs
    else:
        a_ref, w_ref, o_ref = refs
    w = w_ref[...].astype(BF16)
    if out_t:
        y = _dot_nt(w, a_ref[...])
    elif w_t:
        y = _dot_nt(a_ref[...], w)
    else:
        y = _dot(a_ref[...], w)
    if has_bias:
        y = y + b_ref[...]
    if act == "tanh":
        y = jnp.tanh(y)
    elif act == "sigmoid":
        y = _sigmoid(y)
    elif act == "rwkv_decay":
        y = _rwkv_decay(y)
    o_ref[...] = y.astype(o_ref.dtype)


MM_BLOCK_BUDGET = 40 * MIB
MM_VMEM_LIMIT_MIB = 52


def _mm_tiles(rows, k, n, w_itemsize, out_itemsize):
    def fits(tm, tn):
        return 2 * (tm * k * 2 + k * tn * w_itemsize + tm * tn * out_itemsize) <= MM_BLOCK_BUDGET

    tms = [t for t in (2048, 1024, 512, 256, 128, 64, 32, 16) if t <= rows and rows % t == 0] or [rows]
    tns = [t for t in (2048, 1024, 512, 256, 128) if t <= n and n % t == 0] or [n]
    if n <= tns[0] and 2 * k * n * w_itemsize <= MM_BLOCK_BUDGET // 2:
        for tm in tms:
            if fits(tm, n):
                return tm, n
    for tm in tms:
        for tn in tns:
            if fits(tm, tn):
                return tm, tn
    return tms[-1], tns[-1]


def matmul(a, w, widx=(), bias=None, act=None, out_dtype=F32, tb=None, w_t=False):
    m, k = a.shape
    n = w.shape[-2] if w_t else w.shape[-1]
    assert (w.shape[-1] if w_t else w.shape[-2]) == k and (tb is None or w_t)
    tm, tn = _mm_tiles(m if tb is None else tb[1], k, n, w.dtype.itemsize, jnp.dtype(out_dtype).itemsize)
    nj = n // tn
    if tb is None:
        out_shape = (m, n)
        out_block = (tm, tn)
        out_index = lambda i, j: (i, j)
    else:
        bn, t_len = tb
        nt = t_len // tm
        out_shape = (bn * n, t_len)
        out_block = (tn, tm)
        out_index = lambda i, j: ((i // nt) * nj + j, i % nt)
    lead = (None,) * len(widx)
    in_specs = [pl.BlockSpec((tm, k), lambda i, j: (i, 0))]
    if w_t:
        in_specs.append(pl.BlockSpec(lead + (tn, k), lambda i, j: widx + (j, 0)))
    else:
        in_specs.append(pl.BlockSpec(lead + (k, tn), lambda i, j: widx + (0, j)))
    args = [a, w]
    if bias is not None:
        if tb is None:
            in_specs.append(pl.BlockSpec((1, tn), lambda i, j: (0, j)))
            args.append(bias.reshape(1, n))
        else:
            in_specs.append(pl.BlockSpec((tn, 1), lambda i, j: (j, 0)))
            args.append(bias.reshape(n, 1))
    return pl.pallas_call(
        functools.partial(_mm_kernel, act=act, has_bias=bias is not None, w_t=w_t, out_t=tb is not None),
        grid=(m // tm, nj),
        in_specs=in_specs,
        out_specs=pl.BlockSpec(out_block, out_index),
        out_shape=jax.ShapeDtypeStruct(out_shape, out_dtype),
        compiler_params=_cparams(("parallel", "parallel"), MM_VMEM_LIMIT_MIB),
        name="matmul",
    )(*args)


def _mm_post_kernel(*refs, n_a, has_gate, emit_h, transpose_in):
    refs = list(refs)
    a_refs = [refs.pop(0) for _ in range(n_a)]
    gate_ref = refs.pop(0) if has_gate else None
    w_refs = [refs.pop(0) for _ in range(n_a)]
    g1_ref, xres_ref, gn_ref, x_out = refs[:4]
    y = None
    for a_ref, w_ref in zip(a_refs, w_refs):
        a = a_ref[...]
        if has_gate:
            a = a * gate_ref[...]
        a = a.astype(BF16)
        part = _dot_tn(a, w_ref[...]) if transpose_in else _dot(a, w_ref[...])
        y = part if y is None else y + part
    x_new = xres_ref[...] + _rms(y, g1_ref[...])
    x_out[...] = x_new
    if emit_h:
        refs[4][...] = _rms(x_new, gn_ref[...]).astype(BF16)


def matmul_post(a, w, li, g_post, xres, g_next, gate=None, emit_h=True, tb=None):
    a_list = list(a) if isinstance(a, (list, tuple)) else [a]
    n_a = len(a_list)
    assert n_a == 1 or (gate is None and tb is None)
    m, d = xres.shape
    k = w.shape[1] // n_a
    tm = 256 if gate is not None else 512
    tm = min(m, tm) if tb is None else min(tb[1], tm)
    row = lambda i: (i, 0)
    fixed = lambda i: (0, 0)
    if tb is None:
        a_spec = pl.BlockSpec((tm, k), row)
    else:
        nt = tb[1] // tm
        a_spec = pl.BlockSpec((k, tm), lambda i: (i // nt, i % nt))
    in_specs = [a_spec] * n_a
    args = a_list
    if gate is not None:
        in_specs.append(a_spec)
        args.append(gate)
    in_specs += [pl.BlockSpec((None, k, d), lambda i, j=j: (li, j, 0)) for j in range(n_a)]
    in_specs += [pl.BlockSpec((1, d), fixed), pl.BlockSpec((tm, d), row), pl.BlockSpec((1, d), fixed)]
    args += [w] * n_a + [g_post.reshape(1, d), xres, g_next.reshape(1, d)]
    out_specs = [pl.BlockSpec((tm, d), row)]
    out_shape = [jax.ShapeDtypeStruct((m, d), F32)]
    if emit_h:
        out_specs.append(pl.BlockSpec((tm, d), row))
        out_shape.append(jax.ShapeDtypeStruct((m, d), BF16))
    res = pl.pallas_call(
        functools.partial(_mm_post_kernel, n_a=n_a, has_gate=gate is not None, emit_h=emit_h,
                          transpose_in=tb is not None),
        grid=(m // tm,),
        in_specs=in_specs,
        out_specs=out_specs,
        out_shape=out_shape,
        compiler_params=_cparams(("parallel",), 48),
        name="matmul_post",
    )(*args)
    return (res[0], res[1]) if emit_h else (res[0], None)


def _ffn_kernel(h_ref, wg_ref, wu_ref, wd_ref, g1_ref, xres_ref, gn_ref, *outs, emit_h):
    if emit_h:
        x_out, h_out, acc_ref = outs
    else:
        x_out, acc_ref = outs
    j = pl.program_id(1)

    @pl.when(j == 0)
    def _():
        acc_ref[...] = jnp.zeros_like(acc_ref)

    h = h_ref[...]
    gate = _dot(h, wg_ref[...])
    up = _dot(h, wu_ref[...])
    act = (_silu(gate) * up).astype(BF16)
    acc_ref[...] += _dot(act, wd_ref[...])

    @pl.when(j == pl.num_programs(1) - 1)
    def _():
        x_new = xres_ref[...] + _rms(acc_ref[...], g1_ref[...])
        x_out[...] = x_new
        if emit_h:
            h_out[...] = _rms(x_new, gn_ref[...]).astype(BF16)


def ffn(h, w_up, w_down, layer, g_post, xres, g_next, emit_h=True):
    m, d = h.shape
    f = w_down.shape[1]
    tm = min(m, 512)
    th = 512
    nj = f // th
    row = lambda i, j: (i, 0)
    fixed = lambda i, j: (0, 0)
    in_specs = [pl.BlockSpec((tm, d), row),
                pl.BlockSpec((None, d, th), lambda i, j: (layer, 0, j)),
                pl.BlockSpec((None, d, th), lambda i, j: (layer, 0, j + nj)),
                pl.BlockSpec((None, th, d), lambda i, j: (layer, j, 0)),
                pl.BlockSpec((1, d), fixed),
                pl.BlockSpec((tm, d), row),
                pl.BlockSpec((1, d), fixed)]
    out_specs = [pl.BlockSpec((tm, d), row)]
    out_shape = [jax.ShapeDtypeStruct((m, d), F32)]
    if emit_h:
        out_specs.append(pl.BlockSpec((tm, d), row))
        out_shape.append(jax.ShapeDtypeStruct((m, d), BF16))
    res = pl.pallas_call(
        functools.partial(_ffn_kernel, emit_h=emit_h),
        grid=(m // tm, nj),
        in_specs=in_specs,
        out_specs=out_specs,
        out_shape=out_shape,
        scratch_shapes=[pltpu.VMEM((tm, d), F32)],
        compiler_params=_cparams(("parallel", "arbitrary"), 52),
        name="ffn",
    )(h, w_up, w_up, w_down, g_post.reshape(1, d), xres, g_next.reshape(1, d))
    return (res[0], res[1]) if emit_h else (res[0], None)


def _alibi_slope(shape):
    h = pl.program_id(1)
    expo = (h + 1).astype(F32) * (-8.0 / A_HEADS)
    return jnp.exp2(jnp.full(shape, 1.0, F32) * expo)


def _attn_prompt_kernel(q1, k1, v1, q2, k2, v2, q3, k3, v3, out_ref, o_scr, l_scr):
    t_len = out_ref.shape[0]
    blk = A_BLK
    scale = A_HEAD_DIM ** -0.5
    slope = _alibi_slope((blk, blk))
    qi = lax.broadcasted_iota(jnp.int32, (blk, blk), 0)
    ki = lax.broadcasted_iota(jnp.int32, (blk, blk), 1)
    dist_cur = (qi - ki).astype(F32)
    dist_prev = (qi - ki + blk).astype(F32)
    mask_cur = ki <= qi
    mask_prev = ki >= qi
    neg_inf = jnp.float32(-jnp.inf)

    def rows(start, dil):
        if dil == 1:
            return pl.ds(pl.multiple_of(start, blk), blk)
        return pl.ds(start, blk, stride=dil)

    def one_block(gi, q_ref, k_ref, v_ref, dil, start, prev_start, use_prev):
        q = q_ref[rows(start, dil), :].astype(BF16)
        kc = k_ref[rows(start, dil), :].astype(BF16)
        vc = v_ref[rows(start, dil), :].astype(BF16)
        s_c = _dot_nt(q, kc) * scale - slope * (dist_cur * float(dil))
        s_c = jnp.where(mask_cur, s_c, neg_inf)
        mx = jnp.max(s_c, axis=-1, keepdims=True)
        if use_prev is not None:
            kp = k_ref[rows(prev_start, dil), :].astype(BF16)
            vp = v_ref[rows(prev_start, dil), :].astype(BF16)
            s_p = _dot_nt(q, kp) * scale - slope * (dist_prev * float(dil))
            s_p = jnp.where(mask_prev, s_p + jnp.where(use_prev, 0.0, neg_inf), neg_inf)
            mx = jnp.maximum(mx, jnp.max(s_p, axis=-1, keepdims=True))
        p_c = jnp.exp(s_c - mx)
        den = jnp.sum(p_c, axis=-1, keepdims=True)
        acc = _dot(p_c.astype(BF16), vc)
        if use_prev is not None:
            p_p = jnp.exp(s_p - mx)
            den = den + jnp.sum(p_p, axis=-1, keepdims=True)
            acc = acc + _dot(p_p.astype(BF16), vp)
        o = acc / den
        lse = mx + jnp.log(den)
        o_scr[gi, rows(start, dil), :] = o
        l_scr[gi, rows(start, dil), :] = jnp.broadcast_to(lse, (blk, LANES))

    for gi, (q_ref, k_ref, v_ref) in enumerate(((q1, k1, v1), (q2, k2, v2), (q3, k3, v3))):
        window, dil = A_GROUPS[gi]
        assert window // dil == blk
        m_len = t_len // dil
        nb = m_len // blk
        assert nb * blk * dil == t_len

        def body(idx, carry, gi=gi, q_ref=q_ref, k_ref=k_ref, v_ref=v_ref, dil=dil, nb=nb):
            r = idx // nb
            n = idx - r * nb
            start = n * (blk * dil) + r
            if nb == 1:
                one_block(gi, q_ref, k_ref, v_ref, dil, start, None, None)
            else:
                prev_start = jnp.maximum(n - 1, 0) * (blk * dil) + r
                one_block(gi, q_ref, k_ref, v_ref, dil, start, prev_start, n > 0)
            return carry

        lax.fori_loop(0, dil * nb, body, 0, unroll=16)

    def merge(c, carry):
        sl = pl.ds(pl.multiple_of(c * blk, blk), blk)
        l0, l1, l2 = l_scr[0, sl, :], l_scr[1, sl, :], l_scr[2, sl, :]
        mx = jnp.maximum(jnp.maximum(l0, l1), l2)
        w0, w1, w2 = jnp.exp(l0 - mx), jnp.exp(l1 - mx), jnp.exp(l2 - mx)
        num = w0 * o_scr[0, sl, :] + w1 * o_scr[1, sl, :] + w2 * o_scr[2, sl, :]
        out_ref[sl, :] = (num / (w0 + w1 + w2)).astype(out_ref.dtype)
        return carry

    lax.fori_loop(0, t_len // blk, merge, 0, unroll=2)


def attn_prompt(proj):
    bn, t_len, _ = proj.shape
    hd = A_HEAD_DIM

    def col(cb):
        return pl.BlockSpec((None, t_len, hd), lambda b, h, cb=cb: (b, 0, cb * A_HEADS + h))

    in_specs = [col(3 * g + j) for g in range(3) for j in range(3)]
    return pl.pallas_call(
        _attn_prompt_kernel,
        grid=(bn, A_HEADS),
        in_specs=in_specs,
        out_specs=pl.BlockSpec((None, t_len, hd), lambda b, h: (b, 0, h)),
        out_shape=jax.ShapeDtypeStruct((bn, t_len, A_WIDTH), BF16),
        scratch_shapes=[pltpu.VMEM((3, t_len, hd), F32), pltpu.VMEM((3, t_len, LANES), F32)],
        compiler_params=_cparams(("parallel", "parallel"), 40),
        name="attn_prompt",
    )(*([proj] * 9))


KV_ROWS = 2 * A_HEADS
SAMPLE_CHUNKS = 4


def _attn_sample_kernel(*refs, create):
    cols = refs[0:9]
    chunk_refs = refs[9:12]
    next_refs = refs[12:15]
    out_ref = refs[15]
    buf_refs = refs[16:19]
    m_scr, d_scr, acc_scr, new_scr = refs[19:]
    if create:
        c, n_chunks = pl.program_id(2), pl.num_programs(2)
        live = pl.program_id(0) == pl.num_programs(0) - 1
    else:
        c, n_chunks = pl.program_id(1), pl.num_programs(1)
    last = c == n_chunks - 1
    l_new = out_ref.shape[0]
    shift = l_new * KV_ROWS
    hd = A_HEAD_DIM
    scale = hd ** -0.5
    neg_inf = jnp.float32(-jnp.inf)

    nq = A_HEADS * l_new

    @pl.when(c == 0)
    def _():
        m_scr[...] = jnp.full(m_scr.shape, neg_inf, F32)
        d_scr[...] = jnp.zeros(d_scr.shape, F32)
        acc_scr[...] = jnp.zeros(acc_scr.shape, F32)
        for gi in range(3):
            for kv, src in enumerate(cols[3 * gi + 1:3 * gi + 3]):
                for h in range(A_HEADS):
                    new_scr[gi, pl.ds(kv * A_HEADS + h, l_new, stride=KV_ROWS), :] = src[:, h * hd:(h + 1) * hd]
            if not create:
                buf_refs[gi][...] = new_scr[gi]

    if create:
        for gi in range(3):
            c_ref, n_ref, o_ref = chunk_refs[gi], next_refs[gi], buf_refs[gi]
            rows = c_ref.shape[0]
            tail = jnp.where(live, new_scr[gi], jnp.zeros((shift, hd), F32))
            o_ref[0:rows - shift, :] = c_ref[shift:rows, :]
            o_ref[rows - shift:rows, :] = jnp.where(last, tail, n_ref[...])

    def segment(q_all, buf, pos0, n_buf, window, dil, live):
        rows = buf.shape[0]
        keys_on_value_rows = jnp.concatenate([jnp.zeros((A_HEADS, hd), F32), buf[0:rows - A_HEADS]], axis=0)
        s = _dot_nt(q_all, keys_on_value_rows.astype(BF16))
        qrow = lax.broadcasted_iota(jnp.int32, (nq, rows), 0)
        col = lax.broadcasted_iota(jnp.int32, (nq, rows), 1)
        head = qrow >> _log2(l_new)
        dist = n_buf + (qrow & (l_new - 1)) - (pos0 + (col >> _log2(KV_ROWS)))
        valid = jnp.logical_and(
            jnp.logical_and((col & (KV_ROWS - 1)) == A_HEADS + head, (dist & (dil - 1)) == 0),
            jnp.logical_and(dist >= 0, dist <= window))
        slope = jnp.exp2((head + 1).astype(F32) * (-8.0 / A_HEADS))
        s = jnp.where(valid, s * scale - slope * dist.astype(F32), neg_inf)
        if live is not None:
            s = jnp.where(live, s, neg_inf)
        m_old = m_scr[...]
        m_new = jnp.maximum(m_old, jnp.max(s, axis=-1, keepdims=True))
        alpha = jnp.exp(m_old - m_new)
        p = jnp.exp(s - m_new[:, 0:1])
        d_scr[...] = alpha * d_scr[...] + jnp.sum(p, axis=-1, keepdims=True)
        acc_scr[...] = alpha * acc_scr[...] + _dot(p.astype(BF16), buf.astype(BF16))
        m_scr[...] = m_new

    def attend():
        for gi in range(3):
            window, dil = A_GROUPS[gi]
            q_ref = cols[3 * gi]
            c_ref = chunk_refs[gi]
            p_len = c_ref.shape[0] // KV_ROWS
            n_buf = p_len * SAMPLE_CHUNKS
            q_all = jnp.concatenate([q_ref[:, h * hd:(h + 1) * hd] for h in range(A_HEADS)],
                                    axis=0).astype(BF16)
            segment(q_all, new_scr[gi], n_buf, n_buf, window, dil, c == 0)
            segment(q_all, c_ref[...], c * p_len, n_buf, window, dil, None)

    if create:
        pl.when(live)(attend)
    else:
        attend()

    @pl.when(last)
    def _():
        o = acc_scr[...] / d_scr[...]
        if create:
            o = jnp.where(live, o, 0.0)
        for h in range(A_HEADS):
            out_ref[:, h * hd:(h + 1) * hd] = o[h * l_new:(h + 1) * l_new]


def attn_sample(proj, caches, li, bufs):
    bn, l_new, _ = proj.shape
    hd = A_HEAD_DIM
    shift = l_new * KV_ROWS
    n_ab = caches[0].shape[0]
    create = bufs is None
    if create:
        ids = lambda s, b, c: ((li + 1 + s) % n_ab, b, c)
        grid = (n_ab, bn, SAMPLE_CHUNKS)
        sem = ("arbitrary", "arbitrary", "arbitrary")
    else:
        ids = lambda b, c: (li, b, c)
        grid = (bn, SAMPLE_CHUNKS)
        sem = ("parallel", "arbitrary")

    def spec(block, index):
        return pl.BlockSpec(block, lambda *g: index(*ids(*g)))

    in_specs = [spec((None, l_new, A_WIDTH), lambda s, b, c, cb=3 * g + j: (b, 0, cb))
                for g in range(3) for j in range(3)]
    args = [proj] * 9
    chunk_specs, next_specs, tail_specs, out_shape = [], [], [], []
    for g in range(3):
        rows = caches[g].shape[2] // SAMPLE_CHUNKS
        assert rows % shift == 0 and rows > shift
        per = rows // shift
        n_blk = caches[g].shape[2] // shift
        chunk_specs.append(spec((None, None, rows, hd), lambda s, b, c: (s, b, c, 0)))
        next_specs.append(spec((None, None, shift, hd),
                               lambda s, b, c, per=per, n_blk=n_blk: (s, b, jnp.minimum((c + 1) * per, n_blk - 1), 0)))
        tail_specs.append(spec((None, None, shift, hd), lambda s, b, c, n_blk=n_blk: (s, b, n_blk - 1, 0)))
        out_shape.append(jax.ShapeDtypeStruct(caches[g].shape, F32))
    if create:
        in_specs += chunk_specs + next_specs
        args += list(caches) + list(caches)
        buf_specs, aliases = chunk_specs, {}
    else:
        in_specs += chunk_specs + [pl.BlockSpec(memory_space=pl.ANY)] * 3
        args += list(caches) + list(bufs)
        buf_specs, aliases = tail_specs, {12 + g: 1 + g for g in range(3)}
    if create:
        o_spec = pl.BlockSpec((None, None, l_new, A_WIDTH), lambda s, b, c: (s, b, 0, 0))
        o_shape = jax.ShapeDtypeStruct((n_ab, bn, l_new, A_WIDTH), F32)
    else:
        o_spec = pl.BlockSpec((None, l_new, A_WIDTH), lambda b, c: (b, 0, 0))
        o_shape = jax.ShapeDtypeStruct((bn, l_new, A_WIDTH), F32)
    res = pl.pallas_call(
        functools.partial(_attn_sample_kernel, create=create),
        grid=grid,
        in_specs=in_specs,
        out_specs=[o_spec] + buf_specs,
        out_shape=[o_shape] + out_shape,
        scratch_shapes=[pltpu.VMEM((A_HEADS * l_new, LANES), F32)] * 3 + [pltpu.VMEM((3, shift, hd), F32)],
        input_output_aliases=aliases,
        compiler_params=_cparams(sem, 48),
        name="attn_sample",
    )(*args)
    return (res[0][n_ab - 1] if create else res[0]), list(res[1:])


def _kv_pack_kernel(*refs):
    o_ref = refs[-1]
    n_slabs = (len(refs) - 1) // 2
    tm = o_ref.shape[0] // KV_ROWS
    hd = A_HEAD_DIM
    slab = pl.program_id(0)
    for s in range(n_slabs):
        @pl.when(slab == s)
        def _(s=s):
            for kv, src in enumerate(refs[2 * s:2 * s + 2]):
                for h in range(A_HEADS):
                    o_ref[pl.ds(kv * A_HEADS + h, tm, stride=KV_ROWS), :] = src[:, h * hd:(h + 1) * hd]


def kv_pack(projs, g):
    n_ab = len(projs)
    bn, t_len, _ = projs[0].shape
    hd = A_HEAD_DIM
    w = min(A_GROUPS[g][0], t_len)
    tm = min(w, 128)
    row0 = (t_len - w) // tm
    in_specs, args = [], []
    for s in range(n_ab):
        for kv in range(2):
            def index(slab, b, i, s=s, kv=kv):
                mine = slab == s
                return jnp.where(mine, b, 0), jnp.where(mine, row0 + i, row0), 3 * g + 1 + kv
            in_specs.append(pl.BlockSpec((None, tm, A_WIDTH), index))
            args.append(projs[s])
    return pl.pallas_call(
        _kv_pack_kernel,
        grid=(n_ab, bn, w // tm),
        in_specs=in_specs,
        out_specs=pl.BlockSpec((None, None, tm * KV_ROWS, hd), lambda slab, b, i: (slab, b, i, 0)),
        out_shape=jax.ShapeDtypeStruct((n_ab, bn, w * KV_ROWS, hd), F32),
        compiler_params=_cparams(("parallel", "parallel", "parallel"), 32),
        name="kv_pack",
    )(*args)


def _gla_kernel(bq_ref, bf_ref, bi_ref, bg_ref, lbp_ref, gain_ref, s0_ref,
                o_ref, sfin_ref, st_scr, att_scr, qkb_scr, *, li, cin):
    c = pl.program_id(2)
    ch = B_CHUNK
    last = c == pl.num_programs(2) - 1

    def pad(x):
        if cin == ch:
            return x
        return jnp.concatenate([x, jnp.zeros((ch - cin, x.shape[1]), x.dtype)], axis=0)

    row = lax.broadcasted_iota(jnp.int32, (ch, B_DIM), 0)
    lane = lax.broadcasted_iota(jnp.int32, (B_SUB, ch), 1)
    row_sub = lax.broadcasted_iota(jnp.int32, (B_SUB, B_DIM), 0)
    live = row < cin
    tri = (row >= lax.broadcasted_iota(jnp.int32, (ch, ch), 1)).astype(F32)

    @pl.when(c == 0)
    def _():
        for hj in range(B_HEADS_PER_STEP):
            st_scr[hj] = s0_ref[hj].T

    for hj in range(B_HEADS_PER_STEP):
        hs = slice(hj * B_DIM, (hj + 1) * B_DIM)
        lbp = lbp_ref[:, hs]
        e = jnp.exp(lbp - jnp.max(lbp, axis=0, keepdims=True))
        soft = e / jnp.sum(e, axis=0, keepdims=True)
        lb = jnp.sum(soft[0:li + 1], axis=0, keepdims=True) - soft[0:1]

        q = _silu(pad(bq_ref[:, hs]))
        fgate = lb + (1.0 - lb) * _sigmoid(pad(bf_ref[:, hs]))
        g = jnp.where(live, jnp.log(fgate), 0.0)
        kk = jnp.where(live, 1.0 - fgate, 0.0)
        v16 = pad(bi_ref[:, hs]).astype(BF16)

        b = jnp.dot(tri, g, preferred_element_type=F32, precision=lax.Precision.HIGHEST)
        b_last = b[ch - 1:ch]
        st = st_scr[hj]

        o = _dot_nt((q * jnp.exp(b)).astype(BF16), st.astype(BF16))

        q_scr, kk_scr, b_scr = (qkb_scr.at[hj, j] for j in range(3))
        q_scr[...] = q
        kk_scr[...] = kk
        b_scr[...] = b

        for qb in range(ch // B_SUB):
            r0 = qb * B_SUB
            q_s = q_scr[r0:r0 + B_SUB, :]
            b_s = b_scr[r0:r0 + B_SUB, :]
            kk_s = kk_scr[r0:r0 + B_SUB, :]
            if qb > 0:
                b_ref = b_scr[r0 - 1:r0, :]
                qf = q_s * jnp.exp(b_s - b_ref)
                earlier = row < r0
                kf = jnp.where(earlier, kk_scr[...] * jnp.exp(jnp.where(earlier, b_ref - b_scr[...], 0.0)), 0.0)
                att = _dot_nt(qf.astype(BF16), kf.astype(BF16))
            else:
                att = jnp.zeros((B_SUB, ch), F32)
            for s in range(B_SUB):
                causal = row_sub >= s
                dec = jnp.where(causal, jnp.exp(jnp.where(causal, b_s - b_s[s:s + 1], 0.0)), 0.0)
                colv = jnp.sum(q_s * dec * kk_s[s:s + 1], axis=-1, keepdims=True)
                att = jnp.where(lane == r0 + s, colv, att)
            att_scr[hj, r0:r0 + B_SUB, :] = att

        o = o + _dot(att_scr[hj].astype(BF16), v16)

        kd = kk * jnp.exp(b_last - b)
        st_new = st * jnp.exp(b_last) + _dot_tn(v16, kd.astype(BF16))
        st_scr[hj] = st_new

        on = _rms(o, gain_ref[:, hs]) * _silu(pad(bg_ref[:, hs]))
        o_ref[:, hs] = on[0:cin].astype(o_ref.dtype)

    @pl.when(last)
    def _():
        for hj in range(B_HEADS_PER_STEP):
            sfin_ref[hj] = st_scr[hj].T


def gla(proj, lb_params, b_gain, s0, li):
    bn, t_len, ab_in = proj.shape
    n_ab = lb_params.shape[0]
    cin = min(B_CHUNK, t_len)
    nc = t_len // cin
    base = ab_in // B_DIM - 4 * B_HEADS

    hp = B_HEADS_PER_STEP
    wid = hp * B_DIM
    base = base // hp

    def col(j):
        return pl.BlockSpec((None, cin, wid), lambda b, h, c, j=j: (b, c, base + j * (B_HEADS // hp) + h))

    out_dtype = F32 if cin < 16 else BF16
    return pl.pallas_call(
        functools.partial(_gla_kernel, li=li, cin=cin),
        grid=(bn, B_HEADS // hp, nc),
        in_specs=[col(0), col(1), col(2), col(3),
                  pl.BlockSpec((n_ab, wid), lambda b, h, c: (0, h)),
                  pl.BlockSpec((None, 1, wid), lambda b, h, c: (li, 0, h)),
                  pl.BlockSpec((None, hp, B_DIM, B_DIM), lambda b, h, c: (b, h, 0, 0))],
        out_specs=[pl.BlockSpec((None, cin, wid), lambda b, h, c: (b, c, h)),
                   pl.BlockSpec((None, hp, B_DIM, B_DIM), lambda b, h, c: (b, h, 0, 0))],
        out_shape=[jax.ShapeDtypeStruct((bn, t_len, B_HEADS * B_DIM), out_dtype),
                   jax.ShapeDtypeStruct((bn, B_HEADS, B_DIM, B_DIM), F32)],
        scratch_shapes=[pltpu.VMEM((hp, B_DIM, B_DIM), F32), pltpu.VMEM((hp, B_CHUNK, B_CHUNK), F32),
                        pltpu.VMEM((hp, 3, B_CHUNK, B_DIM), F32)],
        compiler_params=_cparams(("parallel", "parallel", "arbitrary"), 32),
        name="gla",
    )(proj, proj, proj, proj, lb_params, b_gain.reshape(n_ab, 1, -1), s0)


def _mix_kernel(x_ref, xp_ref, sh_ref, g_ref, mu_ref, *outs):
    xm_refs, hl_ref = outs[:6], outs[6]
    i = pl.program_id(1)
    tt = x_ref.shape[0]
    g = g_ref[...]
    h = _rms(x_ref[...], g)
    hp = _rms(xp_ref[...], g)
    first = jnp.where(i == 0, sh_ref[...], hp[7:8])
    rolled = pltpu.roll(h, 1, axis=0)
    row = lax.broadcasted_iota(jnp.int32, h.shape, 0)
    prev = jnp.where(row == 0, first, rolled)
    xx = prev - h
    for j in range(6):
        xm_refs[j][...] = (h + xx * mu_ref[j:j + 1]).astype(BF16)

    @pl.when(i == pl.num_programs(1) - 1)
    def _():
        hl_ref[...] = h[tt - 1:tt]


def rwkv_mix(x, shift, g, mu):
    bn, t_len, d = x.shape
    tt = min(t_len, 256)
    sub = tt // 8
    tile = pl.BlockSpec((None, tt, d), lambda b, i: (b, i, 0))
    *xm, hl = pl.pallas_call(
        _mix_kernel,
        grid=(bn, t_len // tt),
        in_specs=[pl.BlockSpec((None, tt, d), lambda b, i: (b, i, 0)),
                  pl.BlockSpec((None, 8, d), lambda b, i: (b, jnp.maximum(i * sub - 1, 0), 0)),
                  pl.BlockSpec((None, 1, d), lambda b, i: (b, 0, 0)),
                  pl.BlockSpec((1, d), lambda b, i: (0, 0)),
                  pl.BlockSpec((6, d), lambda b, i: (0, 0))],
        out_specs=[tile] * 6 + [pl.BlockSpec((None, 1, d), lambda b, i: (b, 0, 0))],
        out_shape=[jax.ShapeDtypeStruct((bn, t_len, d), BF16)] * 6 + [jax.ShapeDtypeStruct((bn, 1, d), F32)],
        compiler_params=_cparams(("parallel", "arbitrary"), 40),
        name="rwkv_mix",
    )(x, x, shift.reshape(bn, 1, d), g.reshape(1, d), mu)
    return [t.reshape(bn * t_len, d) for t in xm], hl.reshape(bn, d)


def _scan_kernel(r_ref, k_ref, v_ref, w_ref, a_ref, par_ref, s0_ref,
                 o_ref, sfin_ref, st, ops, *, tc):
    c = pl.program_id(1)
    n = C_HEAD_DIM

    @pl.when(c == 0)
    def _():
        st[...] = s0_ref[...]

    def key_row(j, ki):
        return ops[j, pl.ds(ki, 1), :]

    def step(t, carry):
        c_kk, c_ka, c_rk, ln_w, ln_b = (par_ref[j] for j in range(5))
        w = w_ref[t]
        a = a_ref[t]
        kt = k_ref[t]
        kk = kt * c_kk
        nrm = jnp.sqrt(jnp.sum(kk * kk, axis=0, keepdims=True))
        kk = kk / jnp.maximum(nrm, 1e-12)
        k2 = kt * (1.0 + (a - 1.0) * c_ka)
        rt = r_ref[t]
        vt = v_ref[t]
        ops[0] = -kk
        ops[1] = w
        ops[2] = kk * a
        ops[3] = k2
        ops[4] = rt
        sa0 = st[0] * key_row(0, 0)
        sa1 = st[1] * key_row(0, 1)
        for ki in range(2, n, 2):
            sa0 = sa0 + st[ki] * key_row(0, ki)
            sa1 = sa1 + st[ki + 1] * key_row(0, ki + 1)
        sa = sa0 + sa1
        o0 = o1 = None
        for ki in range(n):
            s_n = st[ki] * key_row(1, ki) + sa * key_row(2, ki) + vt * key_row(3, ki)
            st[ki] = s_n
            term = s_n * key_row(4, ki)
            if ki % 2 == 0:
                o0 = term if o0 is None else o0 + term
            else:
                o1 = term if o1 is None else o1 + term
        o = o0 + o1
        mean = jnp.mean(o, axis=0, keepdims=True)
        var = jnp.mean(jnp.square(o - mean), axis=0, keepdims=True)
        o = (o - mean) * lax.rsqrt(var + C_GN_EPS) * ln_w + ln_b
        bonus = jnp.sum(rt * k2 * c_rk, axis=0, keepdims=True)
        o_ref[t] = o + bonus * vt
        return carry

    lax.fori_loop(0, tc, step, 0)

    @pl.when(c == pl.num_programs(1) - 1)
    def _():
        sfin_ref[...] = st[...]


def rwkv_scan(r, k, v, decay, iclr, par, s0):
    t_len, n, bh = r.shape
    tc = min(t_len, 32)
    seq = pl.BlockSpec((tc, n, LANES), lambda l, c: (c, 0, l))
    return pl.pallas_call(
        functools.partial(_scan_kernel, tc=tc),
        grid=(bh // LANES, t_len // tc),
        in_specs=[seq] * 5 + [pl.BlockSpec((5, n, LANES), lambda l, c: (0, 0, l)),
                              pl.BlockSpec((n, n, LANES), lambda l, c: (0, 0, l))],
        out_specs=[seq, pl.BlockSpec((n, n, LANES), lambda l, c: (0, 0, l))],
        out_shape=[jax.ShapeDtypeStruct((t_len, n, bh), F32),
                   jax.ShapeDtypeStruct((n, n, bh), F32)],
        scratch_shapes=[pltpu.VMEM((n, n, LANES), F32), pltpu.VMEM((5, n, LANES), F32)],
        compiler_params=_cparams(("parallel", "arbitrary"), 40),
        name="rwkv_scan",
    )(r, k, v, decay, iclr, par, s0)


def _tiles_time(t_len):
    return t_len % 256 == 0


def _pad_to(x, axis, mult):
    pad = (-x.shape[axis]) % mult
    if pad == 0:
        return x
    widths = [(0, 0)] * x.ndim
    widths[axis] = (0, pad)
    return jnp.pad(x, widths)


def _ab_layer(x, h, li, p, caches, b_state, prev_bufs):
    bn, t_len, d = x.shape
    proj = matmul(h, p["w_in_ab"], (li,)).reshape(bn, t_len, -1)
    if caches is None:
        a_out = attn_prompt(proj)
        new_bufs = (prev_bufs or []) + [proj]
        s0 = jnp.zeros((bn, B_HEADS, B_DIM, B_DIM), F32)
    else:
        a_out, new_bufs = attn_sample(proj, caches, li, prev_bufs)
        a_out = a_out.astype(BF16)
        s0 = b_state[li]
    b_out, s_b = gla(proj, p["b_lower_bounds"], p["b_norm_gain"], s0, li)
    mix_in = [t.astype(BF16).reshape(bn * t_len, -1) for t in (a_out, b_out)]
    return mix_in, new_bufs, s_b


def _rwkv_layer(x, li, g_pre, p, shift, wkv):
    bn, t_len, d = x.shape
    n = C_HEAD_DIM
    hh = d // n
    (xr, xw, xk, xv, xa, xg), h_last = rwkv_mix(x, shift, g_pre, p["c_mu"][li])

    def proj_tm(a, w_t, widx, bias=None, act=None):
        if _tiles_time(t_len):
            return matmul(a, w_t, widx, bias=bias, act=act, tb=(bn, t_len), w_t=True)
        y = matmul(a, w_t, widx, bias=bias, act=act, w_t=True)
        return jnp.swapaxes(y.reshape(bn, t_len, -1), 1, 2).reshape(-1, t_len)

    r = proj_tm(xr, p["c_w_rkv_t"], (li, 0))
    k = proj_tm(xk, p["c_w_rkv_t"], (li, 1))
    v = proj_tm(xv, p["c_w_rkv_t"], (li, 2))
    w_lo = matmul(xw, p["c_w1"], (li,), act="tanh", out_dtype=BF16)
    decay = proj_tm(w_lo, p["c_w2_t"], (li,), bias=p["c_w0"][li], act="rwkv_decay")
    a_lo = matmul(xa, p["c_a1"], (li,), out_dtype=BF16)
    iclr = proj_tm(a_lo, p["c_a2_t"], (li,), bias=p["c_a0"][li], act="sigmoid")
    g_lo = matmul(xg, p["c_g1"], (li,), act="sigmoid", out_dtype=BF16)
    gate = proj_tm(g_lo, p["c_g2_t"], (li,))

    def to_scan(t):
        return jnp.transpose(t.reshape(bn * hh, n, t_len), (2, 1, 0))

    def lanes(vec):
        return jnp.tile(vec.reshape(hh, n).T, (1, bn))

    par = jnp.stack([lanes(p["c_k_k"][li]), lanes(p["c_k_a"][li]), lanes(p["c_r_k"][li].reshape(-1)),
                     lanes(p["c_ln_w"][li]), lanes(p["c_ln_b"][li])])
    s0 = jnp.transpose(wkv, (3, 2, 0, 1)).reshape(n, n, bn * hh)
    o, s_fin = rwkv_scan(*(to_scan(t) for t in (r, k, v, decay, iclr)), par, s0)
    o = jnp.transpose(o, (2, 1, 0)).reshape(bn * d, t_len)
    s_fin = jnp.transpose(s_fin.reshape(n, n, bn, hh), (2, 3, 1, 0))
    return o, gate, s_fin, h_last


def _prep_params(raw):
    to16 = lambda w: w.astype(BF16)

    def to16_t(w):
        return jnp.swapaxes(_pad_to(w, w.ndim - 2, LANES), -1, -2).astype(BF16)

    p = dict(raw)
    for name in ("w_out_ab", "c_w_out", "w_ffn_up", "w_ffn_down"):
        p[name] = to16(raw[name])
    for name in ("c_w_rkv", "c_w2", "c_a2", "c_g2"):
        p[name + "_t"] = to16_t(raw[name])
        del p[name]
    for name in ("c_w1", "c_a1", "c_g1"):
        p[name] = _pad_to(raw[name], 2, LANES)
    return p


def _trunk(x, caches, b_states, c_wkv, c_shift, p):
    bn, t_len, d = x.shape
    m = bn * t_len
    depth = p["norm_gains"].shape[0]
    kv_bufs = None
    if caches is not None:
        caches = [c.reshape(c.shape[0], c.shape[1], -1, c.shape[-1]) for c in caches]
    new_b, new_wkv, new_shift = [], [], []
    h = rmsnorm_rows(x.reshape(m, d), p["norm_gains"][0, 0])
    for layer in range(depth):
        gains = p["norm_gains"][layer]
        li = layer // 2
        last = layer == depth - 1
        next_pre = p["norm_gains"][layer + 1, 0] if not last else gains[0]
        if layer % 2 == 0:
            mix_in, kv_bufs, s_b = _ab_layer(x, h, li, p, caches, b_states, kv_bufs)
            new_b.append(s_b)
            x2, h2 = matmul_post(mix_in, p["w_out_ab"], li, gains[1], x.reshape(m, d), gains[2])
        else:
            wkv0 = jnp.zeros((bn, d // C_HEAD_DIM, C_HEAD_DIM, C_HEAD_DIM), F32) if c_wkv is None else c_wkv[li]
            sh0 = jnp.zeros((bn, d), F32) if c_shift is None else c_shift[li]
            o, gate, s_wkv, s_shift = _rwkv_layer(x, li, gains[0], p, sh0, wkv0)
            new_wkv.append(s_wkv)
            new_shift.append(s_shift)
            if _tiles_time(t_len):
                tb = (bn, t_len)
            else:
                tb = None
                o, gate = (jnp.swapaxes(t.reshape(bn, d, t_len), 1, 2).reshape(m, d) for t in (o, gate))
            x2, h2 = matmul_post(o, p["c_w_out"], li, gains[1], x.reshape(m, d), gains[2], gate=gate, tb=tb)
        next_even = (layer + 1) % 2 == 0 and not last
        x3, h = ffn(h2, p["w_ffn_up"], p["w_ffn_down"], layer, gains[3], x2, next_pre, emit_h=next_even)
        x = x3.reshape(bn, t_len, d)
    if caches is None:
        kv_bufs = [kv_pack(kv_bufs, g) for g in range(3)]
    a_states = tuple(kv.reshape(kv.shape[0], bn, -1, 2, A_HEADS, A_HEAD_DIM) for kv in kv_bufs)
    return x, a_states, jnp.stack(new_b), jnp.stack(new_wkv), jnp.stack(new_shift)


def kernel(x_prompt, x_sample, cache_a1_kv, cache_a2_kv, cache_a3_kv, state_b, state_c_wkv, state_c_shift, norm_gains, w_in_ab, w_out_ab, b_lower_bounds, b_norm_gain, c_mu, c_w_rkv, c_w0, c_w1, c_w2, c_a0, c_a1, c_a2, c_g1, c_g2, c_k_k, c_k_a, c_r_k, c_ln_w, c_ln_b, c_w_out, w_ffn_up, w_ffn_down):
    p = _prep_params(dict(
        norm_gains=norm_gains, w_in_ab=w_in_ab, w_out_ab=w_out_ab, b_lower_bounds=b_lower_bounds,
        b_norm_gain=b_norm_gain, c_mu=c_mu, c_w_rkv=c_w_rkv, c_w0=c_w0, c_w1=c_w1, c_w2=c_w2, c_a0=c_a0,
        c_a1=c_a1, c_a2=c_a2, c_g1=c_g1, c_g2=c_g2, c_k_k=c_k_k, c_k_a=c_k_a, c_r_k=c_r_k, c_ln_w=c_ln_w,
        c_ln_b=c_ln_b, c_w_out=c_w_out, w_ffn_up=w_ffn_up, w_ffn_down=w_ffn_down))
    y_p, (pa1, pa2, pa3), pb, pwkv, pshift = _trunk(x_prompt, None, None, None, None, p)
    y_s, (sa1, sa2, sa3), sb, swkv, sshift = _trunk(
        x_sample, (cache_a1_kv, cache_a2_kv, cache_a3_kv), state_b, state_c_wkv, state_c_shift, p)
    return (y_p, y_s, pa1, pa2, pa3, pb, pwkv, pshift, sa1, sa2, sa3, sb, swkv, sshift)
```
